```python
import jax, jax.numpy as jnp
from jax import lax
import numpy as np

D_MODEL = 1024
BATCH = 8
SEQ = 4096
DEPTH = 1

CHUNK = 64
D_MIX = D_MODEL
SWA_HEAD_DIM = 64
SWA_HEADS = (D_MIX // 2) // SWA_HEAD_DIM
SWA_KV_HEADS = 2
SWA_WIDTH = SWA_HEADS * SWA_HEAD_DIM
SWA_KV_WIDTH = SWA_KV_HEADS * SWA_HEAD_DIM
WINDOW = 128
WINDOW_CHUNKS = WINDOW // CHUNK
HGRN_HEAD_DIM = 128
HGRN_WIDTH = D_MIX - SWA_WIDTH
HGRN_HEADS = HGRN_WIDTH // HGRN_HEAD_DIM
IN_SIZES = (SWA_WIDTH, SWA_KV_WIDTH, SWA_KV_WIDTH,
            HGRN_WIDTH, HGRN_WIDTH, HGRN_WIDTH, HGRN_WIDTH)
D_IN = sum(IN_SIZES)
IN_SPLITS = [int(v) for v in np.cumsum(IN_SIZES)[:-1]]
MEM_LEN = 256
XATTN_HEADS = 4
XATTN_HEAD_DIM = D_MODEL // XATTN_HEADS
D_FF = ((8 * D_MODEL // 3 + 255) // 256) * 256
RMS_EPS = 1e-6
NEG_INF = -1e30

kernel_name = "hymba_swa_sink_hgrn2_xattn_layer"


def rms_norm(x, g):
    xf = x.astype(jnp.float32)
    y = xf * lax.rsqrt(jnp.mean(xf * xf, axis=-1, keepdims=True) + RMS_EPS)
    return (y * g.astype(jnp.float32)).astype(x.dtype)


def swa_with_sinks(q, k, v, sinks):
    B, T, Hq, Dh = q.shape
    Hkv = k.shape[2]
    G = Hq // Hkv
    NC = T // CHUNK
    WC = WINDOW_CHUNKS
    L = (WC + 1) * CHUNK
    qc = q.reshape(B, NC, CHUNK, Hkv, G, Dh)
    pad = ((0, 0), (WC * CHUNK, 0), (0, 0), (0, 0))
    kc = jnp.pad(k, pad).reshape(B, NC + WC, CHUNK, Hkv, Dh)
    vc = jnp.pad(v, pad).reshape(B, NC + WC, CHUNK, Hkv, Dh)
    kband = jnp.concatenate([kc[:, j:j + NC] for j in range(WC + 1)], axis=2)
    vband = jnp.concatenate([vc[:, j:j + NC] for j in range(WC + 1)], axis=2)
    band_chunk = jnp.arange(NC)[:, None] - WC + jnp.arange(WC + 1)[None, :]
    valid = jnp.repeat(band_chunk >= 0, CHUNK, axis=1)
    s = jnp.einsum('bnqhgd,bnkhd->bnhgqk', qc, kband).astype(jnp.float32) * (Dh ** -0.5)
    s = jnp.where(valid[None, :, None, None, None, :], s, NEG_INF)
    sink = jnp.broadcast_to(sinks.astype(jnp.float32).reshape(1, 1, Hkv, G, 1, 1),
                            (B, NC, Hkv, G, CHUNK, 1))
    p = jax.nn.softmax(jnp.concatenate([s, sink], axis=-1), axis=-1)[..., :L]
    o = jnp.einsum('bnhgqk,bnkhd->bnqhgd', p.astype(v.dtype), vband)
    return o.reshape(B, T, Hq * Dh)


def hgrn2(q, f_logit, i, g, lb, onorm_g):
    B, T, H, Dk = q.shape
    Dv = i.shape[-1]
    NC = T // CHUNK
    f32 = jnp.float32
    qf = jax.nn.silu(q.astype(f32)) * (Dk ** -0.5)
    lbf = lb.astype(f32)
    f = lbf + (1.0 - lbf) * jax.nn.sigmoid(f_logit.astype(f32))
    kf = 1.0 - f
    logf = jnp.log(f)

    def chunks(a):
        return a.reshape(B, NC, CHUNK, H, a.shape[-1]).transpose(0, 3, 1, 2, 4)

    qc, kc, vc, lc = chunks(qf), chunks(kf), chunks(i.astype(f32)), chunks(logf)
    b = jnp.cumsum(lc, axis=3)
    b_mid = b[:, :, :, CHUNK // 2 - 1:CHUNK // 2]
    b_last = b[:, :, :, CHUNK - 1:CHUNK]
    A = jnp.einsum('bhnqd,bhnkd->bhnqk', qc * jnp.exp(b - b_mid), kc * jnp.exp(b_mid - b))
    causal = jnp.tril(jnp.ones((CHUNK, CHUNK), dtype=bool))
    A = jnp.where(causal, A, 0.0)
    o_intra = jnp.einsum('bhnqk,bhnkv->bhnqv', A, vc)
    kv = jnp.einsum('bhnkd,bhnkv->bhndv', kc * jnp.exp(b_last - b), vc)
    decay = jnp.exp(b_last[:, :, :, 0, :])

    def step(S, inp):
        d, u = inp
        return d[..., None] * S + u, S

    S0 = jnp.zeros((B, H, Dk, Dv), f32)
    _, S_prev = lax.scan(step, S0, (jnp.moveaxis(decay, 2, 0), jnp.moveaxis(kv, 2, 0)))
    S_prev = jnp.moveaxis(S_prev, 0, 2)
    o_inter = jnp.einsum('bhnqd,bhndv->bhnqv', qc * jnp.exp(b), S_prev)
    o = (o_intra + o_inter).transpose(0, 2, 3, 1, 4).reshape(B, T, H, Dv)
    o = rms_norm(o, onorm_g) * jax.nn.silu(g.astype(f32))
    return o.reshape(B, T, H * Dv).astype(q.dtype)


def cross_attention(u, m, wq, wk, wv, wo):
    B, T, _ = u.shape
    M = m.shape[1]
    q = (u @ wq).reshape(B, T, XATTN_HEADS, XATTN_HEAD_DIM)
    k = (m @ wk).reshape(B, M, XATTN_HEADS, XATTN_HEAD_DIM)
    v = (m @ wv).reshape(B, M, XATTN_HEADS, XATTN_HEAD_DIM)
    s = jnp.einsum('bthd,bmhd->bhtm', q, k).astype(jnp.float32) * (XATTN_HEAD_DIM ** -0.5)
    p = jax.nn.softmax(s, axis=-1).astype(v.dtype)
    o = jnp.einsum('bhtm,bmhd->bthd', p, v).reshape(B, T, D_MODEL)
    return o @ wo


def setup_inputs(seed: int = 0) -> dict:
    key = jax.random.key(seed)
    ks = jax.random.split(key, 24)
    f32 = jnp.float32

    def nrm(k, shape, scale):
        return jax.random.normal(k, shape, f32) * scale

    def gain(k, shape):
        return 1.0 + 0.05 * jax.random.normal(k, shape, f32)

    return {
        "x": nrm(ks[0], (BATCH, SEQ, D_MODEL), 1.0),
        "mem": nrm(ks[1], (BATCH, MEM_LEN, D_MODEL), 1.0),
        "w_in": nrm(ks[2], (DEPTH, D_MODEL, D_IN), D_MODEL ** -0.5),
        "sinks": nrm(ks[3], (DEPTH, SWA_HEADS), 0.5),
        "hgrn_lb": nrm(ks[4], (DEPTH + 1, HGRN_WIDTH), 0.1),
        "hgrn_onorm": gain(ks[5], (DEPTH, HGRN_HEAD_DIM)),
        "w_out": nrm(ks[6], (DEPTH, D_MIX, D_MODEL), D_MIX ** -0.5),
        "g_mix_pre": gain(ks[7], (DEPTH, D_MODEL)),
        "g_mix_post": gain(ks[8], (DEPTH, D_MODEL)),
        "g_mem": gain(ks[9], (DEPTH, D_MODEL)),
        "g_x_pre": gain(ks[10], (DEPTH, D_MODEL)),
        "g_x_post": gain(ks[11], (DEPTH, D_MODEL)),
        "wq_x": nrm(ks[12], (DEPTH, D_MODEL, D_MODEL), D_MODEL ** -0.5),
        "wk_x": nrm(ks[13], (DEPTH, D_MODEL, D_MODEL), D_MODEL ** -0.5),
        "wv_x": nrm(ks[14], (DEPTH, D_MODEL, D_MODEL), D_MODEL ** -0.5),
        "wo_x": nrm(ks[15], (DEPTH, D_MODEL, D_MODEL), D_MODEL ** -0.5),
        "g_ffn_pre": gain(ks[16], (DEPTH, D_MODEL)),
        "g_ffn_post": gain(ks[17], (DEPTH, D_MODEL)),
        "w_gate": nrm(ks[18], (DEPTH, D_MODEL, D_FF), D_MODEL ** -0.5),
        "w_up": nrm(ks[19], (DEPTH, D_MODEL, D_FF), D_MODEL ** -0.5),
        "w_down": nrm(ks[20], (DEPTH, D_FF, D_MODEL), D_FF ** -0.5),
    }


def reference(x, mem, w_in, sinks, hgrn_lb, hgrn_onorm, w_out, g_mix_pre, g_mix_post,
              g_mem, g_x_pre, g_x_post, wq_x, wk_x, wv_x, wo_x, g_ffn_pre, g_ffn_post,
              w_gate, w_up, w_down):
    B, T, _ = x.shape
    lb_all = jnp.cumsum(jax.nn.softmax(hgrn_lb.astype(jnp.float32), axis=0), axis=0)
    h = x
    for l in range(DEPTH):
        u = rms_norm(h, g_mix_pre[l])
        z = u @ w_in[l]
        qa, ka, va, qh, fh, ih, gh = jnp.split(z, IN_SPLITS, axis=-1)
        ya = swa_with_sinks(qa.reshape(B, T, SWA_HEADS, SWA_HEAD_DIM),
                            ka.reshape(B, T, SWA_KV_HEADS, SWA_HEAD_DIM),
                            va.reshape(B, T, SWA_KV_HEADS, SWA_HEAD_DIM),
                            sinks[l])
        hv = HGRN_WIDTH // HGRN_HEADS
        yh = hgrn2(qh.reshape(B, T, HGRN_HEADS, HGRN_HEAD_DIM),
                   fh.reshape(B, T, HGRN_HEADS, HGRN_HEAD_DIM),
                   ih.reshape(B, T, HGRN_HEADS, hv),
                   gh.reshape(B, T, HGRN_HEADS, hv),
                   lb_all[l].reshape(HGRN_HEADS, HGRN_HEAD_DIM),
                   hgrn_onorm[l])
        y = jnp.concatenate([ya, yh.astype(ya.dtype)], axis=-1) @ w_out[l]
        h = h + rms_norm(y, g_mix_post[l])
        u = rms_norm(h, g_x_pre[l])
        m = rms_norm(mem, g_mem[l])
        y = cross_attention(u, m, wq_x[l], wk_x[l], wv_x[l], wo_x[l])
        h = h + rms_norm(y, g_x_post[l])
        u = rms_norm(h, g_ffn_pre[l])
        y = (jax.nn.silu(u @ w_gate[l]) * (u @ w_up[l])) @ w_down[l]
        h = h + rms_norm(y, g_ffn_post[l])
    return h
```

```python
import functools

import jax
import jax.numpy as jnp
from jax import lax
from jax.experimental import pallas as pl
from jax.experimental.pallas import tpu as pltpu

D_MODEL = 1024
CHUNK = 64
SWA_HEAD_DIM = 64
SWA_HEADS = 8
SWA_KV_HEADS = 2
SWA_WIDTH = SWA_HEADS * SWA_HEAD_DIM
SWA_KV_WIDTH = SWA_KV_HEADS * SWA_HEAD_DIM
WINDOW = 128
HGRN_HEAD_DIM = 128
HGRN_WIDTH = 512
HGRN_HEADS = HGRN_WIDTH // HGRN_HEAD_DIM
D_IN = SWA_WIDTH + 2 * SWA_KV_WIDTH + 4 * HGRN_WIDTH
HGRN_COL0 = SWA_WIDTH + 2 * SWA_KV_WIDTH
MEM_LEN = 256
XATTN_HEADS = 4
XATTN_HEAD_DIM = D_MODEL // XATTN_HEADS
D_FF = 2816
RMS_EPS = 1e-6
NEG_INF = -1e30

LANES = 128
Q_BLOCK = 128
TOKEN_TILE = 512
VMEM_LIMIT_BYTES = 56 * 1024 * 1024

F32 = jnp.float32
BF16 = jnp.bfloat16


def _rms_scale(x):
    return lax.rsqrt(jnp.mean(x * x, axis=-1, keepdims=True) + RMS_EPS)


def _sigmoid(x):
    return 1.0 / (1.0 + jnp.exp(-x))


def _dot(a, b):
    return jnp.dot(a, b, preferred_element_type=F32)


def _dot_nt(a, b):
    return lax.dot_general(a, b, (((1,), (1,)), ((), ())), preferred_element_type=F32)


def _dot_tn(a, b):
    return lax.dot_general(a, b, (((0,), (0,)), ((), ())), preferred_element_type=F32)


def _mem_kv_kernel(mem_ref, g_ref, wk_ref, wv_ref, kt_ref, v_ref):
    m = mem_ref[0]
    mn = (m * _rms_scale(m) * g_ref[...]).astype(BF16)
    k = _dot(mn, wk_ref[...])
    kt_ref[0] = jnp.transpose(k).astype(BF16)
    v_ref[0] = _dot(mn, wv_ref[...]).astype(BF16)


def _mixer_kernel(sinks_ref, x_ref, lbraw_ref, onorm_ref, gpre_ref, gpost_ref, w_in_ref, w_out_ref,
                  out_ref,
                  kbuf, vbuf, zq_ref, y_ref, qf_ref, kf_ref, b_ref, iv_ref, sg_ref, st_ref):
    t = pl.program_id(1)
    tt = x_ref.shape[1]

    x = x_ref[0]
    u = (x * _rms_scale(x) * gpre_ref[...]).astype(BF16)

    zq_ref[...] = (_dot(u, w_in_ref[:, 0:SWA_WIDTH]) * (SWA_HEAD_DIM ** -0.5)).astype(BF16)
    zkv = _dot(u, w_in_ref[:, SWA_WIDTH:HGRN_COL0])
    k = zkv[:, 0:SWA_KV_WIDTH]
    v = zkv[:, SWA_KV_WIDTH:2 * SWA_KV_WIDTH]

    @pl.when(t == 0)
    def _():
        kbuf[:, 0:WINDOW, :] = jnp.zeros((2, WINDOW, LANES), BF16)
        vbuf[:, 0:WINDOW, :] = jnp.zeros((2, WINDOW, LANES), BF16)
        st_ref[...] = jnp.zeros_like(st_ref)

    kbuf[0, WINDOW:, :] = k.astype(BF16)
    kbuf[1, WINDOW:, :] = pltpu.roll(k, SWA_HEAD_DIM, axis=1).astype(BF16)
    vbuf[0, WINDOW:, :] = v.astype(BF16)
    vbuf[1, WINDOW:, :] = pltpu.roll(v, SWA_HEAD_DIM, axis=1).astype(BF16)

    lane = lax.broadcasted_iota(jnp.int32, (Q_BLOCK, LANES), 1)
    low_half = lane < SWA_HEAD_DIM
    n_keys = WINDOW + Q_BLOCK
    row_chunk = lax.broadcasted_iota(jnp.int32, (Q_BLOCK, n_keys), 0) // CHUNK + WINDOW // CHUNK
    col = lax.broadcasted_iota(jnp.int32, (Q_BLOCK, n_keys), 1)
    col_chunk = col // CHUNK
    band = (col_chunk <= row_chunk) & (col_chunk >= row_chunk - WINDOW // CHUNK)
    pair_blocks = SWA_WIDTH // LANES
    pairs_per_kv = pair_blocks // SWA_KV_HEADS

    def swa_block(j, carry):
        r0 = pl.multiple_of(j * Q_BLOCK, Q_BLOCK)
        started = (t * tt + j * Q_BLOCK) > 0
        valid = band & ((col >= WINDOW) | started)
        qb = zq_ref[pl.ds(r0, Q_BLOCK), :]
        halves = [[None, None] for _ in range(pair_blocks)]
        for ver in range(2):
            parts = []
            for bi in range(pair_blocks):
                parity = ver ^ (bi // pairs_per_kv)
                keep = low_half if parity == 0 else jnp.logical_not(low_half)
                parts.append(jnp.where(keep, qb[:, bi * LANES:(bi + 1) * LANES], jnp.zeros((), BF16)))
            qs = jnp.concatenate(parts, axis=0)
            s = _dot_nt(qs, kbuf[ver, pl.ds(r0, n_keys), :])
            ps, inv_ls = [], []
            for bi in range(pair_blocks):
                parity = ver ^ (bi // pairs_per_kv)
                sink = sinks_ref[2 * bi + parity]
                sb = jnp.where(valid, s[bi * Q_BLOCK:(bi + 1) * Q_BLOCK], NEG_INF)
                m = jnp.maximum(jnp.max(sb, axis=-1, keepdims=True), sink)
                p = jnp.exp(sb - m)
                l = jnp.sum(p, axis=-1, keepdims=True) + jnp.exp(sink - m)
                ps.append(p.astype(BF16))
                inv_ls.append(1.0 / l)
            o = _dot(jnp.concatenate(ps, axis=0), vbuf[ver, pl.ds(r0, n_keys), :])
            for bi in range(pair_blocks):
                parity = ver ^ (bi // pairs_per_kv)
                halves[bi][parity] = o[bi * Q_BLOCK:(bi + 1) * Q_BLOCK] * inv_ls[bi]
        for bi in range(pair_blocks):
            blk = jnp.where(low_half, halves[bi][0], halves[bi][1])
            y_ref[pl.ds(r0, Q_BLOCK), bi * LANES:(bi + 1) * LANES] = blk.astype(BF16)
        return carry

    lax.fori_loop(0, tt // Q_BLOCK, swa_block, 0)

    kbuf[:, 0:WINDOW, :] = kbuf[:, tt:tt + WINDOW, :]
    vbuf[:, 0:WINDOW, :] = vbuf[:, tt:tt + WINDOW, :]

    zh = _dot(u, w_in_ref[:, HGRN_COL0:D_IN])
    a = lbraw_ref[...]
    e = jnp.exp(a - jnp.max(a, axis=0, keepdims=True))
    lb = e[0:1] / jnp.sum(e, axis=0, keepdims=True)

    qh = zh[:, 0:HGRN_WIDTH]
    qf_ref[...] = qh * _sigmoid(qh) * (HGRN_HEAD_DIM ** -0.5)
    f = lb + (1.0 - lb) * _sigmoid(zh[:, HGRN_WIDTH:2 * HGRN_WIDTH])
    kf_ref[...] = 1.0 - f
    logf = jnp.log(f)
    iv_ref[...] = zh[:, 2 * HGRN_WIDTH:3 * HGRN_WIDTH].astype(BF16)
    gh = zh[:, 3 * HGRN_WIDTH:4 * HGRN_WIDTH]
    sg_ref[...] = gh * _sigmoid(gh)

    cs = 4 * CHUNK
    ri = lax.broadcasted_iota(jnp.int32, (cs, cs), 0)
    ci = lax.broadcasted_iota(jnp.int32, (cs, cs), 1)
    tri = jnp.where((ri // CHUNK == ci // CHUNK) & (ci <= ri), 1.0, 0.0).astype(BF16)
    for rb in range(tt // cs):
        lf = logf[rb * cs:(rb + 1) * cs]
        hi = lf.astype(BF16)
        r1 = lf - hi.astype(F32)
        mid = r1.astype(BF16)
        lo = (r1 - mid.astype(F32)).astype(BF16)
        b_ref[rb * cs:(rb + 1) * cs, :] = _dot(tri, hi) + _dot(tri, mid) + _dot(tri, lo)

    ri64 = lax.broadcasted_iota(jnp.int32, (CHUNK, CHUNK), 0)
    ci64 = lax.broadcasted_iota(jnp.int32, (CHUNK, CHUNK), 1)
    causal = ci64 <= ri64
    onorm = onorm_ref[...]

    def hgrn_chunk(c, carry):
        r0 = pl.multiple_of(c * CHUNK, CHUNK)
        rows = pl.ds(r0, CHUNK)
        for hd in range(HGRN_HEADS):
            cols = slice(hd * HGRN_HEAD_DIM, (hd + 1) * HGRN_HEAD_DIM)
            b = b_ref[rows, cols]
            qf = qf_ref[rows, cols]
            kf = kf_ref[rows, cols]
            iv = iv_ref[rows, cols]
            b_mid = b[CHUNK // 2 - 1:CHUNK // 2]
            b_last = b[CHUNK - 1:CHUNK]
            qe = (qf * jnp.exp(b - b_mid)).astype(BF16)
            ke = (kf * jnp.exp(b_mid - b)).astype(BF16)
            amat = jnp.where(causal, _dot_nt(qe, ke), 0.0).astype(BF16)
            st = st_ref[hd]
            qb = (qf * jnp.exp(b)).astype(BF16)
            o = _dot(amat, iv) + _dot_nt(qb, st.astype(BF16))
            kd = (kf * jnp.exp(b_last - b)).astype(BF16)
            st_ref[hd] = jnp.exp(b_last) * st + _dot_tn(iv, kd)
            on = o * _rms_scale(o) * onorm
            y_ref[rows, SWA_WIDTH + hd * HGRN_HEAD_DIM:SWA_WIDTH + (hd + 1) * HGRN_HEAD_DIM] = (
                on * sg_ref[rows, cols]).astype(BF16)
        return carry

    lax.fori_loop(0, tt // CHUNK, hgrn_chunk, 0)

    y = _dot(y_ref[...], w_out_ref[...])
    out_ref[0] = x + y * _rms_scale(y) * gpost_ref[...]


def _xattn_kernel(h_ref, kt_ref, v_ref, gpre_ref, gpost_ref, wq_ref, wo_ref, out_ref, o_ref):
    h = h_ref[0]
    u = (h * _rms_scale(h) * gpre_ref[...]).astype(BF16)
    q = (_dot(u, wq_ref[...]) * (XATTN_HEAD_DIM ** -0.5)).astype(BF16)
    for hd in range(XATTN_HEADS):
        cols = slice(hd * XATTN_HEAD_DIM, (hd + 1) * XATTN_HEAD_DIM)
        s = _dot(q[:, cols], kt_ref[0, cols, :])
        m = jnp.max(s, axis=-1, keepdims=True)
        p = jnp.exp(s - m)
        l = jnp.sum(p, axis=-1, keepdims=True)
        o = _dot(p.astype(BF16), v_ref[0, :, cols]) * (1.0 / l)
        o_ref[:, cols] = o.astype(BF16)
    y = _dot(o_ref[...], wo_ref[...])
    out_ref[0] = h + y * _rms_scale(y) * gpost_ref[...]


def _ffn_kernel(h_ref, gpre_ref, gpost_ref, wg_ref, wu_ref, wd_ref, out_ref):
    h = h_ref[0]
    u = (h * _rms_scale(h) * gpre_ref[...]).astype(BF16)
    gate = _dot(u, wg_ref[...])
    up = _dot(u, wu_ref[...])
    act = (gate * _sigmoid(gate) * up).astype(BF16)
    y = _dot(act, wd_ref[...])
    out_ref[0] = h + y * _rms_scale(y) * gpost_ref[...]


def _resident(shape):
    return pl.BlockSpec(shape, lambda *_: (0,) * len(shape), pipeline_mode=pl.Buffered(1))


def _params(n_axes):
    return pltpu.CompilerParams(dimension_semantics=("arbitrary",) * n_axes,
                                vmem_limit_bytes=VMEM_LIMIT_BYTES)


def kernel(x, mem, w_in, sinks, hgrn_lb, hgrn_onorm, w_out, g_mix_pre, g_mix_post, g_mem, g_x_pre,
           g_x_post, wq_x, wk_x, wv_x, wo_x, g_ffn_pre, g_ffn_post, w_gate, w_up, w_down):
    B, T, D = x.shape
    assert D == D_MODEL and T % TOKEN_TILE == 0 and mem.shape == (B, MEM_LEN, D_MODEL)
    assert w_in.shape == (1, D_MODEL, D_IN) and hgrn_lb.shape[0] == 2
    tt = TOKEN_TILE
    nt = T // tt

    def row(g):
        return g.reshape(1, -1).astype(F32)

    tok_spec = pl.BlockSpec((1, tt, D_MODEL), lambda b, t: (b, t, 0))
    tok_shape = jax.ShapeDtypeStruct((B, T, D_MODEL), F32)

    kt, v = pl.pallas_call(
        _mem_kv_kernel,
        grid=(B,),
        in_specs=[pl.BlockSpec((1, MEM_LEN, D_MODEL), lambda b: (b, 0, 0)),
                  _resident((1, D_MODEL)), _resident((D_MODEL, D_MODEL)), _resident((D_MODEL, D_MODEL))],
        out_specs=[pl.BlockSpec((1, D_MODEL, MEM_LEN), lambda b: (b, 0, 0)),
                   pl.BlockSpec((1, MEM_LEN, D_MODEL), lambda b: (b, 0, 0))],
        out_shape=[jax.ShapeDtypeStruct((B, D_MODEL, MEM_LEN), BF16),
                   jax.ShapeDtypeStruct((B, MEM_LEN, D_MODEL), BF16)],
        compiler_params=_params(1),
        name="mem_kv",
    )(mem, row(g_mem[0]), wk_x[0].astype(BF16), wv_x[0].astype(BF16))

    h1 = pl.pallas_call(
        _mixer_kernel,
        grid=(B, nt),
        in_specs=[pl.BlockSpec(memory_space=pltpu.SMEM),
                  tok_spec,
                  _resident((2, HGRN_WIDTH)), _resident((1, HGRN_HEAD_DIM)),
                  _resident((1, D_MODEL)), _resident((1, D_MODEL)),
                  _resident((D_MODEL, D_IN)), _resident((D_MODEL, D_MODEL))],
        out_specs=tok_spec,
        out_shape=tok_shape,
        scratch_shapes=[
            pltpu.VMEM((2, WINDOW + tt, LANES), BF16),
            pltpu.VMEM((2, WINDOW + tt, LANES), BF16),
            pltpu.VMEM((tt, SWA_WIDTH), BF16),
            pltpu.VMEM((tt, D_MODEL), BF16),
            pltpu.VMEM((tt, HGRN_WIDTH), F32),
            pltpu.VMEM((tt, HGRN_WIDTH), F32),
            pltpu.VMEM((tt, HGRN_WIDTH), F32),
            pltpu.VMEM((tt, HGRN_WIDTH), BF16),
            pltpu.VMEM((tt, HGRN_WIDTH), F32),
            pltpu.VMEM((HGRN_HEADS, HGRN_HEAD_DIM, HGRN_HEAD_DIM), F32),
        ],
        compiler_params=_params(2),
        name="mixer",
    )(sinks[0].astype(F32), x, hgrn_lb.astype(F32), row(hgrn_onorm[0]), row(g_mix_pre[0]),
      row(g_mix_post[0]), w_in[0].astype(BF16), w_out[0].astype(BF16))

    h2 = pl.pallas_call(
        _xattn_kernel,
        grid=(B, nt),
        in_specs=[tok_spec,
                  pl.BlockSpec((1, D_MODEL, MEM_LEN), lambda b, t: (b, 0, 0)),
                  pl.BlockSpec((1, MEM_LEN, D_MODEL), lambda b, t: (b, 0, 0)),
                  _resident((1, D_MODEL)), _resident((1, D_MODEL)),
                  _resident((D_MODEL, D_MODEL)), _resident((D_MODEL, D_MODEL))],
        out_specs=tok_spec,
        out_shape=tok_shape,
        scratch_shapes=[pltpu.VMEM((tt, D_MODEL), BF16)],
        compiler_params=_params(2),
        name="xattn",
    )(h1, kt, v, row(g_x_pre[0]), row(g_x_post[0]), wq_x[0].astype(BF16), wo_x[0].astype(BF16))

    h3 = pl.pallas_call(
        _ffn_kernel,
        grid=(B, nt),
        in_specs=[tok_spec,
                  _resident((1, D_MODEL)), _resident((1, D_MODEL)),
                  _resident((D_MODEL, D_FF)), _resident((D_MODEL, D_FF)), _resident((D_FF, D_MODEL))],
        out_specs=tok_spec,
        out_shape=tok_shape,
        compiler_params=_params(2),
        name="ffn",
    )(h2, row(g_ffn_pre[0]), row(g_ffn_post[0]), w_gate[0].astype(BF16), w_up[0].astype(BF16),
      w_down[0].astype(BF16))
    return h3
```

```python
import functools

import jax
import jax.numpy as jnp
from jax import lax
from jax.experimental import pallas as pl
from jax.experimental.pallas import tpu as pltpu

D_MODEL = 1024
CHUNK = 64
SWA_HEAD_DIM = 64
SWA_HEADS = 8
SWA_KV_HEADS = 2
SWA_WIDTH = SWA_HEADS * SWA_HEAD_DIM
SWA_KV_WIDTH = SWA_KV_HEADS * SWA_HEAD_DIM
WINDOW = 128
HGRN_HEAD_DIM = 128
HGRN_WIDTH = 512
HGRN_HEADS = HGRN_WIDTH // HGRN_HEAD_DIM
D_IN = SWA_WIDTH + 2 * SWA_KV_WIDTH + 4 * HGRN_WIDTH
HGRN_COL0 = SWA_WIDTH + 2 * SWA_KV_WIDTH
MEM_LEN = 256
XATTN_HEADS = 4
XATTN_HEAD_DIM = D_MODEL // XATTN_HEADS
D_FF = 2816
RMS_EPS = 1e-6
NEG_INF = -1e30

LANES = 128
Q_BLOCK = 128
TOKEN_TILE = 512
VMEM_LIMIT_BYTES = 56 * 1024 * 1024

F32 = jnp.float32
BF16 = jnp.bfloat16


def _rms_scale(x):
    return lax.rsqrt(jnp.mean(x * x, axis=-1, keepdims=True) + RMS_EPS)


def _sigmoid(x):
    return 1.0 / (1.0 + jnp.exp(-x))


def _dot(a, b):
    return jnp.dot(a, b, preferred_element_type=F32)


def _dot_nt(a, b):
    return lax.dot_general(a, b, (((1,), (1,)), ((), ())), preferred_element_type=F32)


def _dot_tn(a, b):
    return lax.dot_general(a, b, (((0,), (0,)), ((), ())), preferred_element_type=F32)


def _mem_kv_kernel(mem_ref, g_ref, wk_ref, wv_ref, kt_ref, v_ref):
    m = mem_ref[0]
    mn = (m * _rms_scale(m) * g_ref[...]).astype(BF16)
    k = _dot(mn, wk_ref[...])
    kt_ref[0] = jnp.transpose(k).astype(BF16)
    v_ref[0] = _dot(mn, wv_ref[...]).astype(BF16)


def _mixer_kernel(sinks_ref, x_ref, lbraw_ref, onorm_ref, gpre_ref, gpost_ref, w_in_ref, w_out_ref,
                  out_ref,
                  kbuf, vbuf, zq_ref, y_ref, qf_ref, kf_ref, b_ref, iv_ref, sg_ref, st_ref):
    t = pl.program_id(1)
    tt = x_ref.shape[1]

    x = x_ref[0]
    u = (x * _rms_scale(x) * gpre_ref[...]).astype(BF16)

    zq_ref[...] = (_dot(u, w_in_ref[:, 0:SWA_WIDTH]) * (SWA_HEAD_DIM ** -0.5)).astype(BF16)
    zkv = _dot(u, w_in_ref[:, SWA_WIDTH:HGRN_COL0])
    k = zkv[:, 0:SWA_KV_WIDTH]
    v = zkv[:, SWA_KV_WIDTH:2 * SWA_KV_WIDTH]

    @pl.when(t == 0)
    def _():
        kbuf[:, 0:WINDOW, :] = jnp.zeros((2, WINDOW, LANES), BF16)
        vbuf[:, 0:WINDOW, :] = jnp.zeros((2, WINDOW, LANES), BF16)
        st_ref[...] = jnp.zeros_like(st_ref)

    kbuf[0, WINDOW:, :] = k.astype(BF16)
    kbuf[1, WINDOW:, :] = pltpu.roll(k, SWA_HEAD_DIM, axis=1).astype(BF16)
    vbuf[0, WINDOW:, :] = v.astype(BF16)
    vbuf[1, WINDOW:, :] = pltpu.roll(v, SWA_HEAD_DIM, axis=1).astype(BF16)

    lane = lax.broadcasted_iota(jnp.int32, (Q_BLOCK, LANES), 1)
    low_half = lane < SWA_HEAD_DIM
    n_keys = WINDOW + Q_BLOCK
    row_chunk = lax.broadcasted_iota(jnp.int32, (Q_BLOCK, n_keys), 0) // CHUNK + WINDOW // CHUNK
    col = lax.broadcasted_iota(jnp.int32, (Q_BLOCK, n_keys), 1)
    col_chunk = col // CHUNK
    band = (col_chunk <= row_chunk) & (col_chunk >= row_chunk - WINDOW // CHUNK)
    pair_blocks = SWA_WIDTH // LANES
    pairs_per_kv = pair_blocks // SWA_KV_HEADS

    def swa_block(j, carry):
        r0 = j * Q_BLOCK
        started = (t * tt + j * Q_BLOCK) > 0
        valid = band & ((col >= WINDOW) | started)
        qb = zq_ref[pl.ds(r0, Q_BLOCK), :]
        halves = [[None, None] for _ in range(pair_blocks)]
        for ver in range(2):
            parts = []
            for bi in range(pair_blocks):
                parity = ver ^ (bi // pairs_per_kv)
                keep = low_half if parity == 0 else jnp.logical_not(low_half)
                parts.append(jnp.where(keep, qb[:, bi * LANES:(bi + 1) * LANES], jnp.zeros((), BF16)))
            qs = jnp.concatenate(parts, axis=0)
            s = _dot_nt(qs, kbuf[ver, pl.ds(r0, n_keys), :])
            ps, inv_ls = [], []
            for bi in range(pair_blocks):
                parity = ver ^ (bi // pairs_per_kv)
                sink = sinks_ref[2 * bi + parity]
                sb = jnp.where(valid, s[bi * Q_BLOCK:(bi + 1) * Q_BLOCK], NEG_INF)
                m = jnp.maximum(jnp.max(sb, axis=-1, keepdims=True), sink)
                p = jnp.exp(sb - m)
                l = jnp.sum(p, axis=-1, keepdims=True) + jnp.exp(sink - m)
                ps.append(p.astype(BF16))
                inv_ls.append(1.0 / l)
            o = _dot(jnp.concatenate(ps, axis=0), vbuf[ver, pl.ds(r0, n_keys), :])
            for bi in range(pair_blocks):
                parity = ver ^ (bi // pairs_per_kv)
                halves[bi][parity] = o[bi * Q_BLOCK:(bi + 1) * Q_BLOCK] * inv_ls[bi]
        for bi in range(pair_blocks):
            blk = jnp.where(low_half, halves[bi][0], halves[bi][1])
            y_ref[pl.ds(r0, Q_BLOCK), bi * LANES:(bi + 1) * LANES] = blk.astype(BF16)
        return carry

    for j in range(tt // Q_BLOCK):
        swa_block(j, 0)

    kbuf[:, 0:WINDOW, :] = kbuf[:, tt:tt + WINDOW, :]
    vbuf[:, 0:WINDOW, :] = vbuf[:, tt:tt + WINDOW, :]

    zh = _dot(u, w_in_ref[:, HGRN_COL0:D_IN])
    a = lbraw_ref[...]
    e = jnp.exp(a - jnp.max(a, axis=0, keepdims=True))
    lb = e[0:1] / jnp.sum(e, axis=0, keepdims=True)

    qh = zh[:, 0:HGRN_WIDTH]
    qf_ref[...] = qh * _sigmoid(qh) * (HGRN_HEAD_DIM ** -0.5)
    f = lb + (1.0 - lb) * _sigmoid(zh[:, HGRN_WIDTH:2 * HGRN_WIDTH])
    kf_ref[...] = 1.0 - f
    logf = jnp.log(f)
    iv_ref[...] = zh[:, 2 * HGRN_WIDTH:3 * HGRN_WIDTH].astype(BF16)
    gh = zh[:, 3 * HGRN_WIDTH:4 * HGRN_WIDTH]
    sg_ref[...] = gh * _sigmoid(gh)

    cs = 4 * CHUNK
    ri = lax.broadcasted_iota(jnp.int32, (cs, cs), 0)
    ci = lax.broadcasted_iota(jnp.int32, (cs, cs), 1)
    tri = jnp.where((ri // CHUNK == ci // CHUNK) & (ci <= ri), 1.0, 0.0).astype(BF16)
    for rb in range(tt // cs):
        lf = logf[rb * cs:(rb + 1) * cs]
        hi = lf.astype(BF16)
        r1 = lf - hi.astype(F32)
        mid = r1.astype(BF16)
        lo = (r1 - mid.astype(F32)).astype(BF16)
        b_ref[rb * cs:(rb + 1) * cs, :] = _dot(tri, hi) + _dot(tri, mid) + _dot(tri, lo)

    ri64 = lax.broadcasted_iota(jnp.int32, (CHUNK, CHUNK), 0)
    ci64 = lax.broadcasted_iota(jnp.int32, (CHUNK, CHUNK), 1)
    causal = ci64 <= ri64
    onorm = onorm_ref[...]

    def hgrn_chunk(c, carry):
        r0 = c * CHUNK
        rows = pl.ds(r0, CHUNK)
        for hd in range(HGRN_HEADS):
            cols = slice(hd * HGRN_HEAD_DIM, (hd + 1) * HGRN_HEAD_DIM)
            b = b_ref[rows, cols]
            qf = qf_ref[rows, cols]
            kf = kf_ref[rows, cols]
            iv = iv_ref[rows, cols]
            b_mid = b[CHUNK // 2 - 1:CHUNK // 2]
            b_last = b[CHUNK - 1:CHUNK]
            qe = (qf * jnp.exp(b - b_mid)).astype(BF16)
            ke = (kf * jnp.exp(b_mid - b)).astype(BF16)
            amat = jnp.where(causal, _dot_nt(qe, ke), 0.0).astype(BF16)
            st = st_ref[hd]
            qb = (qf * jnp.exp(b)).astype(BF16)
            o = _dot(amat, iv) + _dot_nt(qb, st.astype(BF16))
            kd = (kf * jnp.exp(b_last - b)).astype(BF16)
            st_ref[hd] = jnp.exp(b_last) * st + _dot_tn(iv, kd)
            on = o * _rms_scale(o) * onorm
            y_ref[rows, SWA_WIDTH + hd * HGRN_HEAD_DIM:SWA_WIDTH + (hd + 1) * HGRN_HEAD_DIM] = (
                on * sg_ref[rows, cols]).astype(BF16)
        return carry

    for c in range(tt // CHUNK):
        hgrn_chunk(c, 0)

    y = _dot(y_ref[...], w_out_ref[...])
    out_ref[0] = x + y * _rms_scale(y) * gpost_ref[...]


def _xattn_kernel(h_ref, kt_ref, v_ref, gpre_ref, gpost_ref, wq_ref, wo_ref, out_ref, o_ref):
    h = h_ref[0]
    u = (h * _rms_scale(h) * gpre_ref[...]).astype(BF16)
    q = (_dot(u, wq_ref[...]) * (XATTN_HEAD_DIM ** -0.5)).astype(BF16)
    for hd in range(XATTN_HEADS):
        cols = slice(hd * XATTN_HEAD_DIM, (hd + 1) * XATTN_HEAD_DIM)
        s = _dot(q[:, cols], kt_ref[0, cols, :])
        m = jnp.max(s, axis=-1, keepdims=True)
        p = jnp.exp(s - m)
        l = jnp.sum(p, axis=-1, keepdims=True)
        o = _dot(p.astype(BF16), v_ref[0, :, cols]) * (1.0 / l)
        o_ref[:, cols] = o.astype(BF16)
    y = _dot(o_ref[...], wo_ref[...])
    out_ref[0] = h + y * _rms_scale(y) * gpost_ref[...]


def _ffn_kernel(h_ref, gpre_ref, gpost_ref, wg_ref, wu_ref, wd_ref, out_ref):
    h = h_ref[0]
    u = (h * _rms_scale(h) * gpre_ref[...]).astype(BF16)
    gate = _dot(u, wg_ref[...])
    up = _dot(u, wu_ref[...])
    act = (gate * _sigmoid(gate) * up).astype(BF16)
    y = _dot(act, wd_ref[...])
    out_ref[0] = h + y * _rms_scale(y) * gpost_ref[...]


def _resident(shape):
    return pl.BlockSpec(shape, lambda *_: (0,) * len(shape), pipeline_mode=pl.Buffered(1))


def _params(n_axes):
    return pltpu.CompilerParams(dimension_semantics=("arbitrary",) * n_axes,
                                vmem_limit_bytes=VMEM_LIMIT_BYTES)


def kernel(x, mem, w_in, sinks, hgrn_lb, hgrn_onorm, w_out, g_mix_pre, g_mix_post, g_mem, g_x_pre,
           g_x_post, wq_x, wk_x, wv_x, wo_x, g_ffn_pre, g_ffn_post, w_gate, w_up, w_down):
    B, T, D = x.shape
    assert D == D_MODEL and T % TOKEN_TILE == 0 and mem.shape == (B, MEM_LEN, D_MODEL)
    assert w_in.shape == (1, D_MODEL, D_IN) and hgrn_lb.shape[0] == 2
    tt = TOKEN_TILE
    nt = T // tt

    def row(g):
        return g.reshape(1, -1).astype(F32)

    tok_spec = pl.BlockSpec((1, tt, D_MODEL), lambda b, t: (b, t, 0))
    tok_shape = jax.ShapeDtypeStruct((B, T, D_MODEL), F32)

    kt, v = pl.pallas_call(
        _mem_kv_kernel,
        grid=(B,),
        in_specs=[pl.BlockSpec((1, MEM_LEN, D_MODEL), lambda b: (b, 0, 0)),
                  _resident((1, D_MODEL)), _resident((D_MODEL, D_MODEL)), _resident((D_MODEL, D_MODEL))],
        out_specs=[pl.BlockSpec((1, D_MODEL, MEM_LEN), lambda b: (b, 0, 0)),
                   pl.BlockSpec((1, MEM_LEN, D_MODEL), lambda b: (b, 0, 0))],
        out_shape=[jax.ShapeDtypeStruct((B, D_MODEL, MEM_LEN), BF16),
                   jax.ShapeDtypeStruct((B, MEM_LEN, D_MODEL), BF16)],
        compiler_params=_params(1),
        name="mem_kv",
    )(mem, row(g_mem[0]), wk_x[0].astype(BF16), wv_x[0].astype(BF16))

    h1 = pl.pallas_call(
        _mixer_kernel,
        grid=(B, nt),
        in_specs=[pl.BlockSpec(memory_space=pltpu.SMEM),
                  tok_spec,
                  _resident((2, HGRN_WIDTH)), _resident((1, HGRN_HEAD_DIM)),
                  _resident((1, D_MODEL)), _resident((1, D_MODEL)),
                  _resident((D_MODEL, D_IN)), _resident((D_MODEL, D_MODEL))],
        out_specs=tok_spec,
        out_shape=tok_shape,
        scratch_shapes=[
            pltpu.VMEM((2, WINDOW + tt, LANES), BF16),
            pltpu.VMEM((2, WINDOW + tt, LANES), BF16),
            pltpu.VMEM((tt, SWA_WIDTH), BF16),
            pltpu.VMEM((tt, D_MODEL), BF16),
            pltpu.VMEM((tt, HGRN_WIDTH), F32),
            pltpu.VMEM((tt, HGRN_WIDTH), F32),
            pltpu.VMEM((tt, HGRN_WIDTH), F32),
            pltpu.VMEM((tt, HGRN_WIDTH), BF16),
            pltpu.VMEM((tt, HGRN_WIDTH), F32),
            pltpu.VMEM((HGRN_HEADS, HGRN_HEAD_DIM, HGRN_HEAD_DIM), F32),
        ],
        compiler_params=_params(2),
        name="mixer",
    )(sinks[0].astype(F32), x, hgrn_lb.astype(F32), row(hgrn_onorm[0]), row(g_mix_pre[0]),
      row(g_mix_post[0]), w_in[0].astype(BF16), w_out[0].astype(BF16))

    h2 = pl.pallas_call(
        _xattn_kernel,
        grid=(B, nt),
        in_specs=[tok_spec,
                  pl.BlockSpec((1, D_MODEL, MEM_LEN), lambda b, t: (b, 0, 0)),
                  pl.BlockSpec((1, MEM_LEN, D_MODEL), lambda b, t: (b, 0, 0)),
                  _resident((1, D_MODEL)), _resident((1, D_MODEL)),
                  _resident((D_MODEL, D_MODEL)), _resident((D_MODEL, D_MODEL))],
        out_specs=tok_spec,
        out_shape=tok_shape,
        scratch_shapes=[pltpu.VMEM((tt, D_MODEL), BF16)],
        compiler_params=_params(2),
        name="xattn",
    )(h1, kt, v, row(g_x_pre[0]), row(g_x_post[0]), wq_x[0].astype(BF16), wo_x[0].astype(BF16))

    h3 = pl.pallas_call(
        _ffn_kernel,
        grid=(B, nt),
        in_specs=[tok_spec,
                  _resident((1, D_MODEL)), _resident((1, D_MODEL)),
                  _resident((D_MODEL, D_FF)), _resident((D_MODEL, D_FF)), _resident((D_FF, D_MODEL))],
        out_specs=tok_spec,
        out_shape=tok_shape,
        compiler_params=_params(2),
        name="ffn",
    )(h2, row(g_ffn_pre[0]), row(g_ffn_post[0]), w_gate[0].astype(BF16), w_up[0].astype(BF16),
      w_down[0].astype(BF16))
    return h3
```

```python
import collections
import functools

import jax
import jax.numpy as jnp
from jax import lax
from jax.experimental import pallas as pl
from jax.experimental.pallas import tpu as pltpu

D_MODEL = 1024
CHUNK = 64
SWA_HEAD_DIM = 64
SWA_HEADS = 8
SWA_KV_HEADS = 2
SWA_WIDTH = SWA_HEADS * SWA_HEAD_DIM
SWA_KV_WIDTH = SWA_KV_HEADS * SWA_HEAD_DIM
WINDOW = 128
HGRN_HEAD_DIM = 128
HGRN_WIDTH = 512
HGRN_HEADS = HGRN_WIDTH // HGRN_HEAD_DIM
D_IN = SWA_WIDTH + 2 * SWA_KV_WIDTH + 4 * HGRN_WIDTH
HGRN_COL0 = SWA_WIDTH + 2 * SWA_KV_WIDTH
MEM_LEN = 256
XATTN_HEADS = 4
XATTN_HEAD_DIM = D_MODEL // XATTN_HEADS
D_FF = 2816
RMS_EPS = 1e-6
NEG_INF = -1e30

LANES = 128
Q_BLOCK = 128
TOKEN_TILE = 512
VMEM_LIMIT_BYTES = 56 * 1024 * 1024

F32 = jnp.float32
BF16 = jnp.bfloat16


def _rms_scale(x):
    return lax.rsqrt(jnp.mean(x * x, axis=-1, keepdims=True) + RMS_EPS)


def _sigmoid(x):
    return 1.0 / (1.0 + jnp.exp(-x))


def _dot(a, b):
    return jnp.dot(a, b, preferred_element_type=F32)


def _dot_nt(a, b):
    return lax.dot_general(a, b, (((1,), (1,)), ((), ())), preferred_element_type=F32)


def _dot_tn(a, b):
    return lax.dot_general(a, b, (((0,), (0,)), ((), ())), preferred_element_type=F32)


def _mem_kv_kernel(mem_ref, g_ref, wk_ref, wv_ref, kt_ref, v_ref):
    m = mem_ref[0]
    mn = (m * _rms_scale(m) * g_ref[...]).astype(BF16)
    k = _dot(mn, wk_ref[...])
    kt_ref[0] = jnp.transpose(k).astype(BF16)
    v_ref[0] = _dot(mn, wv_ref[...]).astype(BF16)


ProjSet = collections.namedtuple("ProjSet", "zq k v qf kf b iv sg")


def _proj_set_shapes(tt):
    return [
        pltpu.VMEM((tt, SWA_WIDTH), BF16),
        pltpu.VMEM((2, WINDOW + tt, LANES), BF16),
        pltpu.VMEM((2, WINDOW + tt, LANES), BF16),
        pltpu.VMEM((tt, HGRN_WIDTH), F32),
        pltpu.VMEM((tt, HGRN_WIDTH), F32),
        pltpu.VMEM((tt, HGRN_WIDTH), F32),
        pltpu.VMEM((tt, HGRN_WIDTH), BF16),
        pltpu.VMEM((tt, HGRN_WIDTH), F32),
    ]


class _ProjectStage:
    def __init__(self, x_ref, gpre_ref, lbraw_ref, w_in_ref, ps):
        self.x_ref, self.gpre_ref, self.lbraw_ref, self.w_in_ref, self.ps = (
            x_ref, gpre_ref, lbraw_ref, w_in_ref, ps)
        self.tt = x_ref.shape[1]

    def _hgrn_cols(self, i):
        return _dot(self.u, self.w_in_ref[:, HGRN_COL0 + i * HGRN_WIDTH:HGRN_COL0 + (i + 1) * HGRN_WIDTH])

    def norm(self):
        x = self.x_ref[0]
        self.u = (x * _rms_scale(x) * self.gpre_ref[...]).astype(BF16)

    def swa_q(self):
        zq = _dot(self.u, self.w_in_ref[:, 0:SWA_WIDTH])
        self.ps.zq[...] = (zq * (SWA_HEAD_DIM ** -0.5)).astype(BF16)

    def swa_kv(self):
        ps = self.ps
        zkv = _dot(self.u, self.w_in_ref[:, SWA_WIDTH:HGRN_COL0])
        k = zkv[:, 0:SWA_KV_WIDTH]
        v = zkv[:, SWA_KV_WIDTH:2 * SWA_KV_WIDTH]
        ps.k[0, WINDOW:, :] = k.astype(BF16)
        ps.k[1, WINDOW:, :] = pltpu.roll(k, SWA_HEAD_DIM, axis=1).astype(BF16)
        ps.v[0, WINDOW:, :] = v.astype(BF16)
        ps.v[1, WINDOW:, :] = pltpu.roll(v, SWA_HEAD_DIM, axis=1).astype(BF16)

    def hgrn_q(self):
        qh = self._hgrn_cols(0)
        self.ps.qf[...] = qh * _sigmoid(qh) * (HGRN_HEAD_DIM ** -0.5)

    def hgrn_f(self):
        a = self.lbraw_ref[...]
        e = jnp.exp(a - jnp.max(a, axis=0, keepdims=True))
        lb = e[0:1] / jnp.sum(e, axis=0, keepdims=True)
        f = lb + (1.0 - lb) * _sigmoid(self._hgrn_cols(1))
        self.ps.kf[...] = 1.0 - f
        self.logf = jnp.log(f)

    def hgrn_i(self):
        self.ps.iv[...] = self._hgrn_cols(2).astype(BF16)

    def hgrn_g(self):
        gh = self._hgrn_cols(3)
        self.ps.sg[...] = gh * _sigmoid(gh)

    def hgrn_cumsum(self):
        cs = 4 * CHUNK
        ri = lax.broadcasted_iota(jnp.int32, (cs, cs), 0)
        ci = lax.broadcasted_iota(jnp.int32, (cs, cs), 1)
        tri = jnp.where((ri // CHUNK == ci // CHUNK) & (ci <= ri), 1.0, 0.0).astype(BF16)
        for rb in range(self.tt // cs):
            lf = self.logf[rb * cs:(rb + 1) * cs]
            hi = lf.astype(BF16)
            r1 = lf - hi.astype(F32)
            mid = r1.astype(BF16)
            lo = (r1 - mid.astype(F32)).astype(BF16)
            self.ps.b[rb * cs:(rb + 1) * cs, :] = _dot(tri, hi) + _dot(tri, mid) + _dot(tri, lo)


class _MixStage:
    PAIR_BLOCKS = SWA_WIDTH // LANES
    PAIRS_PER_KV = PAIR_BLOCKS // SWA_KV_HEADS
    N_KEYS = WINDOW + Q_BLOCK

    def __init__(self, first, x_ref, sinks_ref, onorm_ref, gpost_ref, w_out_ref, ps, kwin, vwin,
                 st_ref, y_ref, out_ref):
        self.first, self.x_ref, self.sinks_ref, self.onorm_ref, self.gpost_ref = (
            first, x_ref, sinks_ref, onorm_ref, gpost_ref)
        self.w_out_ref, self.ps, self.kwin, self.vwin, self.st_ref, self.y_ref, self.out_ref = (
            w_out_ref, ps, kwin, vwin, st_ref, y_ref, out_ref)
        self.tt = x_ref.shape[1]
        self.scores = {}
        self.amat = {}
        self.kvt = {}

    def _parity(self, ver, bi):
        return ver ^ (bi // self.PAIRS_PER_KV)

    def swa_window(self):
        ps = self.ps
        zero_win = jnp.zeros((2, WINDOW, LANES), BF16)
        ps.k[:, 0:WINDOW, :] = jnp.where(self.first, zero_win, self.kwin[...])
        ps.v[:, 0:WINDOW, :] = jnp.where(self.first, zero_win, self.vwin[...])
        lane = lax.broadcasted_iota(jnp.int32, (Q_BLOCK, LANES), 1)
        self.low_half = lane < SWA_HEAD_DIM
        row_chunk = lax.broadcasted_iota(jnp.int32, (Q_BLOCK, self.N_KEYS), 0) // CHUNK + WINDOW // CHUNK
        col = lax.broadcasted_iota(jnp.int32, (Q_BLOCK, self.N_KEYS), 1)
        col_chunk = col // CHUNK
        self.band = (col_chunk <= row_chunk) & (col_chunk >= row_chunk - WINDOW // CHUNK)
        self.band_first = self.band & ((col >= WINDOW) | jnp.logical_not(self.first))

    def swa_scores(self, j):
        r0 = j * Q_BLOCK
        qb = self.ps.zq[r0:r0 + Q_BLOCK, :]
        for ver in range(2):
            parts = []
            for bi in range(self.PAIR_BLOCKS):
                keep = self.low_half if self._parity(ver, bi) == 0 else jnp.logical_not(self.low_half)
                parts.append(jnp.where(keep, qb[:, bi * LANES:(bi + 1) * LANES], jnp.zeros((), BF16)))
            qs = jnp.concatenate(parts, axis=0)
            self.scores[j, ver] = _dot_nt(qs, self.ps.k[ver, r0:r0 + self.N_KEYS, :])

    def swa_finish(self, j):
        r0 = j * Q_BLOCK
        valid = self.band_first if j == 0 else self.band
        halves = [[None, None] for _ in range(self.PAIR_BLOCKS)]
        for ver in range(2):
            s = self.scores.pop((j, ver))
            probs, inv_ls = [], []
            for bi in range(self.PAIR_BLOCKS):
                sink = self.sinks_ref[2 * bi + self._parity(ver, bi)]
                sb = jnp.where(valid, s[bi * Q_BLOCK:(bi + 1) * Q_BLOCK], NEG_INF)
                m = jnp.maximum(jnp.max(sb, axis=-1, keepdims=True), sink)
                p = jnp.exp(sb - m)
                l = jnp.sum(p, axis=-1, keepdims=True) + jnp.exp(sink - m)
                probs.append(p.astype(BF16))
                inv_ls.append(1.0 / l)
            o = _dot(jnp.concatenate(probs, axis=0), self.ps.v[ver, r0:r0 + self.N_KEYS, :])
            for bi in range(self.PAIR_BLOCKS):
                halves[bi][self._parity(ver, bi)] = o[bi * Q_BLOCK:(bi + 1) * Q_BLOCK] * inv_ls[bi]
        for bi in range(self.PAIR_BLOCKS):
            blk = jnp.where(self.low_half, halves[bi][0], halves[bi][1])
            self.y_ref[r0:r0 + Q_BLOCK, bi * LANES:(bi + 1) * LANES] = blk.astype(BF16)

    def swa_save_window(self):
        self.kwin[...] = self.ps.k[:, self.tt:self.tt + WINDOW, :]
        self.vwin[...] = self.ps.v[:, self.tt:self.tt + WINDOW, :]

    def _chunk(self, hd, c):
        ps = self.ps
        rows = slice(c * CHUNK, (c + 1) * CHUNK)
        cols = slice(hd * HGRN_HEAD_DIM, (hd + 1) * HGRN_HEAD_DIM)
        return rows, cols, ps.b[rows, cols], ps.qf[rows, cols], ps.kf[rows, cols], ps.iv[rows, cols]

    def hgrn_local(self, hd):
        ri = lax.broadcasted_iota(jnp.int32, (CHUNK, CHUNK), 0)
        ci = lax.broadcasted_iota(jnp.int32, (CHUNK, CHUNK), 1)
        causal = ci <= ri
        for c in range(self.tt // CHUNK):
            _, _, b, qf, kf, iv = self._chunk(hd, c)
            b_mid = b[CHUNK // 2 - 1:CHUNK // 2]
            b_last = b[CHUNK - 1:CHUNK]
            qe = (qf * jnp.exp(b - b_mid)).astype(BF16)
            ke = (kf * jnp.exp(b_mid - b)).astype(BF16)
            self.amat[hd, c] = jnp.where(causal, _dot_nt(qe, ke), 0.0).astype(BF16)
            kd = (kf * jnp.exp(b_last - b)).astype(BF16)
            self.kvt[hd, c] = _dot_tn(iv, kd)

    def hgrn_state(self, hd):
        onorm = self.onorm_ref[...]
        st = self.st_ref[hd] * jnp.where(self.first, 0.0, 1.0)
        for c in range(self.tt // CHUNK):
            rows, cols, b, qf, _, iv = self._chunk(hd, c)
            qb = (qf * jnp.exp(b)).astype(BF16)
            o = _dot(self.amat.pop((hd, c)), iv) + _dot_nt(qb, st.astype(BF16))
            st = jnp.exp(b[CHUNK - 1:CHUNK]) * st + self.kvt.pop((hd, c))
            on = o * _rms_scale(o) * onorm
            self.y_ref[rows, SWA_WIDTH + hd * HGRN_HEAD_DIM:SWA_WIDTH + (hd + 1) * HGRN_HEAD_DIM] = (
                on * self.ps.sg[rows, cols]).astype(BF16)
        self.st_ref[hd] = st

    def out_swa(self):
        self.y = _dot(self.y_ref[:, 0:SWA_WIDTH], self.w_out_ref[0:SWA_WIDTH, :])

    def out_hgrn(self):
        y = self.y + _dot(self.y_ref[:, SWA_WIDTH:D_MODEL], self.w_out_ref[SWA_WIDTH:D_MODEL, :])
        self.out_ref[0] = self.x_ref[0] + y * _rms_scale(y) * self.gpost_ref[...]


def _mixer_kernel(tiles_per_seq, sinks_ref, xa_ref, xb_ref, lbraw_ref, onorm_ref, gpre_ref, gpost_ref,
                  w_in_ref, w_out_ref, out_ref, *scratch):
    n_set = len(ProjSet._fields)
    sets = (ProjSet(*scratch[0:n_set]), ProjSet(*scratch[n_set:2 * n_set]))
    kwin, vwin, st_ref, y_ref = scratch[2 * n_set:]
    s = pl.program_id(0)

    @pl.when(s == 0)
    def _():
        for ref in sets[1] + (kwin, vwin, st_ref):
            ref[...] = jnp.zeros_like(ref)

    first = lax.rem(jnp.maximum(s - 1, 0), tiles_per_seq) == 0

    def body(write_set, read_set):
        pj = _ProjectStage(xa_ref, gpre_ref, lbraw_ref, w_in_ref, write_set)
        mx = _MixStage(first, xb_ref, sinks_ref, onorm_ref, gpost_ref, w_out_ref, read_set, kwin, vwin,
                       st_ref, y_ref, out_ref)
        mx.swa_window()
        pj.norm()
        mx.swa_scores(0)
        pj.swa_q()
        mx.swa_finish(0)
        mx.swa_scores(1)
        pj.swa_kv()
        mx.swa_finish(1)
        mx.swa_scores(2)
        pj.hgrn_q()
        mx.swa_finish(2)
        mx.swa_scores(3)
        pj.hgrn_f()
        mx.swa_finish(3)
        mx.swa_save_window()
        mx.hgrn_local(0)
        pj.hgrn_i()
        mx.hgrn_state(0)
        mx.hgrn_local(1)
        pj.hgrn_g()
        mx.hgrn_state(1)
        mx.hgrn_local(2)
        pj.hgrn_cumsum()
        mx.hgrn_state(2)
        mx.hgrn_local(3)
        mx.out_swa()
        mx.hgrn_state(3)
        mx.out_hgrn()

    @pl.when(lax.rem(s, 2) == 0)
    def _():
        body(sets[0], sets[1])

    @pl.when(lax.rem(s, 2) == 1)
    def _():
        body(sets[1], sets[0])


def _xattn_kernel(h_ref, kt_ref, v_ref, gpre_ref, gpost_ref, wq_ref, wo_ref, out_ref, o_ref):
    h = h_ref[0]
    u = (h * _rms_scale(h) * gpre_ref[...]).astype(BF16)
    q = (_dot(u, wq_ref[...]) * (XATTN_HEAD_DIM ** -0.5)).astype(BF16)
    for hd in range(XATTN_HEADS):
        cols = slice(hd * XATTN_HEAD_DIM, (hd + 1) * XATTN_HEAD_DIM)
        s = _dot(q[:, cols], kt_ref[0, cols, :])
        m = jnp.max(s, axis=-1, keepdims=True)
        p = jnp.exp(s - m)
        l = jnp.sum(p, axis=-1, keepdims=True)
        o = _dot(p.astype(BF16), v_ref[0, :, cols]) * (1.0 / l)
        o_ref[:, cols] = o.astype(BF16)
    y = _dot(o_ref[...], wo_ref[...])
    out_ref[0] = h + y * _rms_scale(y) * gpost_ref[...]


def _ffn_kernel(h_ref, gpre_ref, gpost_ref, wg_ref, wu_ref, wd_ref, out_ref):
    h = h_ref[0]
    u = (h * _rms_scale(h) * gpre_ref[...]).astype(BF16)
    gate = _dot(u, wg_ref[...])
    up = _dot(u, wu_ref[...])
    act = (gate * _sigmoid(gate) * up).astype(BF16)
    y = _dot(act, wd_ref[...])
    out_ref[0] = h + y * _rms_scale(y) * gpost_ref[...]


def _resident(shape):
    return pl.BlockSpec(shape, lambda *_: (0,) * len(shape), pipeline_mode=pl.Buffered(1))


def _params(n_axes):
    return pltpu.CompilerParams(dimension_semantics=("arbitrary",) * n_axes,
                                vmem_limit_bytes=VMEM_LIMIT_BYTES)


def kernel(x, mem, w_in, sinks, hgrn_lb, hgrn_onorm, w_out, g_mix_pre, g_mix_post, g_mem, g_x_pre,
           g_x_post, wq_x, wk_x, wv_x, wo_x, g_ffn_pre, g_ffn_post, w_gate, w_up, w_down):
    B, T, D = x.shape
    assert D == D_MODEL and T % TOKEN_TILE == 0 and mem.shape == (B, MEM_LEN, D_MODEL)
    assert w_in.shape == (1, D_MODEL, D_IN) and hgrn_lb.shape[0] == 2
    tt = TOKEN_TILE
    nt = T // tt
    n_tiles = B * nt

    def row(g):
        return g.reshape(1, -1).astype(F32)

    tok_spec = pl.BlockSpec((1, tt, D_MODEL), lambda b, t: (b, t, 0))
    tok_shape = jax.ShapeDtypeStruct((B, T, D_MODEL), F32)

    kt, v = pl.pallas_call(
        _mem_kv_kernel,
        grid=(B,),
        in_specs=[pl.BlockSpec((1, MEM_LEN, D_MODEL), lambda b: (b, 0, 0)),
                  _resident((1, D_MODEL)), _resident((D_MODEL, D_MODEL)), _resident((D_MODEL, D_MODEL))],
        out_specs=[pl.BlockSpec((1, D_MODEL, MEM_LEN), lambda b: (b, 0, 0)),
                   pl.BlockSpec((1, MEM_LEN, D_MODEL), lambda b: (b, 0, 0))],
        out_shape=[jax.ShapeDtypeStruct((B, D_MODEL, MEM_LEN), BF16),
                   jax.ShapeDtypeStruct((B, MEM_LEN, D_MODEL), BF16)],
        compiler_params=_params(1),
        name="mem_kv",
    )(mem, row(g_mem[0]), wk_x[0].astype(BF16), wv_x[0].astype(BF16))

    def project_idx(s):
        i = jnp.minimum(s, n_tiles - 1)
        return (i // nt, i % nt, 0)

    def mix_idx(s):
        i = jnp.maximum(s - 1, 0)
        return (i // nt, i % nt, 0)

    h1 = pl.pallas_call(
        functools.partial(_mixer_kernel, nt),
        grid=(n_tiles + 1,),
        in_specs=[pl.BlockSpec(memory_space=pltpu.SMEM),
                  pl.BlockSpec((1, tt, D_MODEL), project_idx),
                  pl.BlockSpec((1, tt, D_MODEL), mix_idx),
                  _resident((2, HGRN_WIDTH)), _resident((1, HGRN_HEAD_DIM)),
                  _resident((1, D_MODEL)), _resident((1, D_MODEL)),
                  _resident((D_MODEL, D_IN)), _resident((D_MODEL, D_MODEL))],
        out_specs=pl.BlockSpec((1, tt, D_MODEL), mix_idx),
        out_shape=tok_shape,
        scratch_shapes=_proj_set_shapes(tt) + _proj_set_shapes(tt) + [
            pltpu.VMEM((2, WINDOW, LANES), BF16),
            pltpu.VMEM((2, WINDOW, LANES), BF16),
            pltpu.VMEM((HGRN_HEADS, HGRN_HEAD_DIM, HGRN_HEAD_DIM), F32),
            pltpu.VMEM((tt, D_MODEL), BF16),
        ],
        compiler_params=_params(1),
        name="mixer",
    )(sinks[0].astype(F32), x, x, hgrn_lb.astype(F32), row(hgrn_onorm[0]), row(g_mix_pre[0]),
      row(g_mix_post[0]), w_in[0].astype(BF16), w_out[0].astype(BF16))

    h2 = pl.pallas_call(
        _xattn_kernel,
        grid=(B, nt),
        in_specs=[tok_spec,
                  pl.BlockSpec((1, D_MODEL, MEM_LEN), lambda b, t: (b, 0, 0)),
                  pl.BlockSpec((1, MEM_LEN, D_MODEL), lambda b, t: (b, 0, 0)),
                  _resident((1, D_MODEL)), _resident((1, D_MODEL)),
                  _resident((D_MODEL, D_MODEL)), _resident((D_MODEL, D_MODEL))],
        out_specs=tok_spec,
        out_shape=tok_shape,
        scratch_shapes=[pltpu.VMEM((tt, D_MODEL), BF16)],
        compiler_params=_params(2),
        name="xattn",
    )(h1, kt, v, row(g_x_pre[0]), row(g_x_post[0]), wq_x[0].astype(BF16), wo_x[0].astype(BF16))

    h3 = pl.pallas_call(
        _ffn_kernel,
        grid=(B, nt),
        in_specs=[tok_spec,
                  _resident((1, D_MODEL)), _resident((1, D_MODEL)),
                  _resident((D_MODEL, D_FF)), _resident((D_MODEL, D_FF)), _resident((D_FF, D_MODEL))],
        out_specs=tok_spec,
        out_shape=tok_shape,
        compiler_params=_params(2),
        name="ffn",
    )(h2, row(g_ffn_pre[0]), row(g_ffn_post[0]), w_gate[0].astype(BF16), w_up[0].astype(BF16),
      w_down[0].astype(BF16))
    return h3
```

```python
import collections
import functools

import jax
import jax.numpy as jnp
from jax import lax
from jax.experimental import pallas as pl
from jax.experimental.pallas import tpu as pltpu

D_MODEL = 1024
CHUNK = 64
SWA_HEAD_DIM = 64
SWA_HEADS = 8
SWA_KV_HEADS = 2
SWA_WIDTH = SWA_HEADS * SWA_HEAD_DIM
SWA_KV_WIDTH = SWA_KV_HEADS * SWA_HEAD_DIM
WINDOW = 128
HGRN_HEAD_DIM = 128
HGRN_WIDTH = 512
HGRN_HEADS = HGRN_WIDTH // HGRN_HEAD_DIM
D_IN = SWA_WIDTH + 2 * SWA_KV_WIDTH + 4 * HGRN_WIDTH
HGRN_COL0 = SWA_WIDTH + 2 * SWA_KV_WIDTH
MEM_LEN = 256
XATTN_HEADS = 4
XATTN_HEAD_DIM = D_MODEL // XATTN_HEADS
D_FF = 2816
RMS_EPS = 1e-6
NEG_INF = -1e30
LOG2E = 1.4426950408889634

LANES = 128
Q_BLOCK = 128
TOKEN_TILE = 512
VMEM_LIMIT_BYTES = 56 * 1024 * 1024

F32 = jnp.float32
BF16 = jnp.bfloat16


def _rms_scale(x):
    return lax.rsqrt(jnp.mean(x * x, axis=-1, keepdims=True) + RMS_EPS)


def _sigmoid(x):
    return 1.0 / (1.0 + jnp.exp2(x * (-LOG2E)))


def _dot(a, b):
    return jnp.dot(a, b, preferred_element_type=F32)


def _dot_nt(a, b):
    return lax.dot_general(a, b, (((1,), (1,)), ((), ())), preferred_element_type=F32)


def _dot_tn(a, b):
    return lax.dot_general(a, b, (((0,), (0,)), ((), ())), preferred_element_type=F32)


def _mem_kv_kernel(mem_ref, g_ref, wk_ref, wv_ref, kt_ref, v_ref):
    m = mem_ref[0]
    mn = (m * _rms_scale(m) * g_ref[...]).astype(BF16)
    k = _dot(mn, wk_ref[...])
    kt_ref[0] = jnp.transpose(k).astype(BF16)
    v_ref[0] = _dot(mn, wv_ref[...]).astype(BF16)


ProjSet = collections.namedtuple("ProjSet", "zq k v qe ke qb kd dec iv sg")


def _proj_set_shapes(tt):
    return [
        pltpu.VMEM((tt, SWA_WIDTH), BF16),
        pltpu.VMEM((2, WINDOW + tt, LANES), BF16),
        pltpu.VMEM((2, WINDOW + tt, LANES), BF16),
        pltpu.VMEM((tt, HGRN_WIDTH), BF16),
        pltpu.VMEM((tt, HGRN_WIDTH), BF16),
        pltpu.VMEM((tt, HGRN_WIDTH), BF16),
        pltpu.VMEM((tt, HGRN_WIDTH), BF16),
        pltpu.VMEM((tt // CHUNK, HGRN_WIDTH), F32),
        pltpu.VMEM((tt, HGRN_WIDTH), BF16),
        pltpu.VMEM((tt, HGRN_WIDTH), F32),
    ]


class _ProjectStage:
    def __init__(self, x_ref, gpre_ref, lbraw_ref, w_in_ref, ps):
        self.x_ref, self.gpre_ref, self.lbraw_ref, self.w_in_ref, self.ps = (
            x_ref, gpre_ref, lbraw_ref, w_in_ref, ps)
        self.tt = x_ref.shape[1]

    def _hgrn_cols(self, i):
        return _dot(self.u, self.w_in_ref[:, HGRN_COL0 + i * HGRN_WIDTH:HGRN_COL0 + (i + 1) * HGRN_WIDTH])

    def norm(self):
        x = self.x_ref[0]
        self.u = (x * _rms_scale(x) * self.gpre_ref[...]).astype(BF16)

    def swa_q(self):
        zq = _dot(self.u, self.w_in_ref[:, 0:SWA_WIDTH])
        self.ps.zq[...] = (zq * (SWA_HEAD_DIM ** -0.5 * LOG2E)).astype(BF16)

    def swa_kv(self):
        ps = self.ps
        zkv = _dot(self.u, self.w_in_ref[:, SWA_WIDTH:HGRN_COL0])
        k = zkv[:, 0:SWA_KV_WIDTH]
        v = zkv[:, SWA_KV_WIDTH:2 * SWA_KV_WIDTH]
        ps.k[0, WINDOW:, :] = k.astype(BF16)
        ps.k[1, WINDOW:, :] = pltpu.roll(k, SWA_HEAD_DIM, axis=1).astype(BF16)
        ps.v[0, WINDOW:, :] = v.astype(BF16)
        ps.v[1, WINDOW:, :] = pltpu.roll(v, SWA_HEAD_DIM, axis=1).astype(BF16)

    def hgrn_q(self):
        qh = self._hgrn_cols(0)
        self.qf = qh * _sigmoid(qh) * (HGRN_HEAD_DIM ** -0.5)

    def hgrn_f(self):
        a = self.lbraw_ref[...]
        e = jnp.exp(a - jnp.max(a, axis=0, keepdims=True))
        lb = e[0:1] / jnp.sum(e, axis=0, keepdims=True)
        f = lb + (1.0 - lb) * _sigmoid(self._hgrn_cols(1))
        self.kf = 1.0 - f
        self.log2f = jnp.log2(f)

    def hgrn_i(self, half):
        w = HGRN_WIDTH // 2
        c0 = HGRN_COL0 + 2 * HGRN_WIDTH + half * w
        self.ps.iv[:, half * w:(half + 1) * w] = _dot(self.u, self.w_in_ref[:, c0:c0 + w]).astype(BF16)

    def hgrn_g(self):
        gh = self._hgrn_cols(3)
        self.ps.sg[...] = gh * _sigmoid(gh)

    def hgrn_decay(self):
        ps = self.ps
        cs = 4 * CHUNK
        ri = lax.broadcasted_iota(jnp.int32, (cs, cs), 0)
        ci = lax.broadcasted_iota(jnp.int32, (cs, cs), 1)
        tri = jnp.where((ri // CHUNK == ci // CHUNK) & (ci <= ri), 1.0, 0.0).astype(BF16)
        for rb in range(self.tt // cs):
            lf = self.log2f[rb * cs:(rb + 1) * cs]
            hi = lf.astype(BF16)
            r1 = lf - hi.astype(F32)
            mid = r1.astype(BF16)
            lo = (r1 - mid.astype(F32)).astype(BF16)
            b_blk = _dot(tri, hi) + _dot(tri, mid) + _dot(tri, lo)
            for cc in range(cs // CHUNK):
                c = rb * (cs // CHUNK) + cc
                rows = slice(c * CHUNK, (c + 1) * CHUNK)
                b = b_blk[cc * CHUNK:(cc + 1) * CHUNK]
                b_mid = b[CHUNK // 2 - 1:CHUNK // 2]
                b_last = b[CHUNK - 1:CHUNK]
                qe = self.qf[rows] * jnp.exp2(b - b_mid)
                ke = self.kf[rows] * jnp.exp2(b_mid - b)
                ps.qe[rows, :] = qe.astype(BF16)
                ps.ke[rows, :] = ke.astype(BF16)
                ps.qb[rows, :] = (qe * jnp.exp2(b_mid)).astype(BF16)
                ps.kd[rows, :] = (ke * jnp.exp2(b_last - b_mid)).astype(BF16)
                ps.dec[c:c + 1, :] = jnp.exp2(b_last)


class _MixStage:
    PAIR_BLOCKS = SWA_WIDTH // LANES
    PAIRS_PER_KV = PAIR_BLOCKS // SWA_KV_HEADS
    N_KEYS = WINDOW + Q_BLOCK

    def __init__(self, first, x_ref, sinks_ref, onorm_ref, gpost_ref, w_out_ref, ps, kwin, vwin,
                 st_ref, y_ref, out_ref):
        self.first, self.x_ref, self.sinks_ref, self.onorm_ref, self.gpost_ref = (
            first, x_ref, sinks_ref, onorm_ref, gpost_ref)
        self.w_out_ref, self.ps, self.kwin, self.vwin, self.st_ref, self.y_ref, self.out_ref = (
            w_out_ref, ps, kwin, vwin, st_ref, y_ref, out_ref)
        self.tt = x_ref.shape[1]
        self.scores = {}
        self.amat = {}
        self.kvt = {}

    def _parity(self, ver, bi):
        return ver ^ (bi // self.PAIRS_PER_KV)

    def swa_window(self):
        ps = self.ps
        zero_win = jnp.zeros((2, WINDOW, LANES), BF16)
        ps.k[:, 0:WINDOW, :] = jnp.where(self.first, zero_win, self.kwin[...])
        ps.v[:, 0:WINDOW, :] = jnp.where(self.first, zero_win, self.vwin[...])
        lane = lax.broadcasted_iota(jnp.int32, (Q_BLOCK, LANES), 1)
        self.low_half = lane < SWA_HEAD_DIM
        row_chunk = lax.broadcasted_iota(jnp.int32, (Q_BLOCK, self.N_KEYS), 0) // CHUNK + WINDOW // CHUNK
        col = lax.broadcasted_iota(jnp.int32, (Q_BLOCK, self.N_KEYS), 1)
        col_chunk = col // CHUNK
        self.band = (col_chunk <= row_chunk) & (col_chunk >= row_chunk - WINDOW // CHUNK)
        self.band_first = self.band & ((col >= WINDOW) | jnp.logical_not(self.first))
        col_q = lax.broadcasted_iota(jnp.int32, (CHUNK, LANES), 1)
        self.keep_low = col_q < CHUNK
        self.keep_high = col_q >= CHUNK

    def _mask_scores(self, j, s):
        if j == 0:
            return jnp.where(self.band_first, s, NEG_INF)
        top = jnp.concatenate([s[0:CHUNK, 0:LANES],
                               jnp.where(self.keep_low, s[0:CHUNK, LANES:2 * LANES], NEG_INF)], axis=1)
        bot = jnp.concatenate([jnp.where(self.keep_high, s[CHUNK:2 * CHUNK, 0:LANES], NEG_INF),
                               s[CHUNK:2 * CHUNK, LANES:2 * LANES]], axis=1)
        return jnp.concatenate([top, bot], axis=0)

    def swa_scores(self, j):
        r0 = j * Q_BLOCK
        qb = self.ps.zq[r0:r0 + Q_BLOCK, :]
        for ver in range(2):
            parts = []
            for bi in range(self.PAIR_BLOCKS):
                keep = self.low_half if self._parity(ver, bi) == 0 else jnp.logical_not(self.low_half)
                parts.append(jnp.where(keep, qb[:, bi * LANES:(bi + 1) * LANES], jnp.zeros((), BF16)))
            qs = jnp.concatenate(parts, axis=0)
            self.scores[j, ver] = _dot_nt(qs, self.ps.k[ver, r0:r0 + self.N_KEYS, :])

    def swa_finish(self, j):
        r0 = j * Q_BLOCK
        halves = [[None, None] for _ in range(self.PAIR_BLOCKS)]
        for ver in range(2):
            s = self.scores.pop((j, ver))
            probs, inv_ls = [], []
            for bi in range(self.PAIR_BLOCKS):
                sink = self.sinks_ref[2 * bi + self._parity(ver, bi)] * LOG2E
                sb = self._mask_scores(j, s[bi * Q_BLOCK:(bi + 1) * Q_BLOCK])
                m = jnp.maximum(jnp.max(sb, axis=-1, keepdims=True), sink)
                p = jnp.exp2(sb - m)
                l = jnp.sum(p, axis=-1, keepdims=True) + jnp.exp2(sink - m)
                probs.append(p.astype(BF16))
                inv_ls.append(1.0 / l)
            o = _dot(jnp.concatenate(probs, axis=0), self.ps.v[ver, r0:r0 + self.N_KEYS, :])
            for bi in range(self.PAIR_BLOCKS):
                halves[bi][self._parity(ver, bi)] = o[bi * Q_BLOCK:(bi + 1) * Q_BLOCK] * inv_ls[bi]
        for bi in range(self.PAIR_BLOCKS):
            blk = jnp.where(self.low_half, halves[bi][0], halves[bi][1])
            self.y_ref[r0:r0 + Q_BLOCK, bi * LANES:(bi + 1) * LANES] = blk.astype(BF16)

    def swa_save_window(self):
        self.kwin[...] = self.ps.k[:, self.tt:self.tt + WINDOW, :]
        self.vwin[...] = self.ps.v[:, self.tt:self.tt + WINDOW, :]

    @staticmethod
    def _chunk(hd, c):
        return slice(c * CHUNK, (c + 1) * CHUNK), slice(hd * HGRN_HEAD_DIM, (hd + 1) * HGRN_HEAD_DIM)

    def hgrn_local(self, hd):
        ps = self.ps
        ri = lax.broadcasted_iota(jnp.int32, (CHUNK, CHUNK), 0)
        ci = lax.broadcasted_iota(jnp.int32, (CHUNK, CHUNK), 1)
        causal = ci <= ri
        for c in range(self.tt // CHUNK):
            rows, cols = self._chunk(hd, c)
            a = _dot_nt(ps.qe[rows, cols], ps.ke[rows, cols])
            self.amat[hd, c] = jnp.where(causal, a, 0.0).astype(BF16)
            self.kvt[hd, c] = _dot_tn(ps.iv[rows, cols], ps.kd[rows, cols])

    def hgrn_state(self, hd):
        ps = self.ps
        onorm = self.onorm_ref[...]
        st = self.st_ref[hd] * jnp.where(self.first, 0.0, 1.0)
        for c in range(self.tt // CHUNK):
            rows, cols = self._chunk(hd, c)
            o = (_dot(self.amat.pop((hd, c)), ps.iv[rows, cols])
                 + _dot_nt(ps.qb[rows, cols], st.astype(BF16)))
            st = ps.dec[c:c + 1, cols] * st + self.kvt.pop((hd, c))
            on = o * _rms_scale(o) * onorm
            self.y_ref[rows, SWA_WIDTH + hd * HGRN_HEAD_DIM:SWA_WIDTH + (hd + 1) * HGRN_HEAD_DIM] = (
                on * self.ps.sg[rows, cols]).astype(BF16)
        self.st_ref[hd] = st

    def out_swa(self):
        self.y = _dot(self.y_ref[:, 0:SWA_WIDTH], self.w_out_ref[0:SWA_WIDTH, :])

    def out_hgrn(self):
        y = self.y + _dot(self.y_ref[:, SWA_WIDTH:D_MODEL], self.w_out_ref[SWA_WIDTH:D_MODEL, :])
        self.out_ref[0] = self.x_ref[0] + y * _rms_scale(y) * self.gpost_ref[...]


def _mixer_kernel(tiles_per_seq, sinks_ref, xa_ref, xb_ref, lbraw_ref, onorm_ref, gpre_ref, gpost_ref,
                  w_in_ref, w_out_ref, out_ref, *scratch):
    n_set = len(ProjSet._fields)
    sets = (ProjSet(*scratch[0:n_set]), ProjSet(*scratch[n_set:2 * n_set]))
    kwin, vwin, st_ref, y_ref = scratch[2 * n_set:]
    s = pl.program_id(0)

    @pl.when(s == 0)
    def _():
        for ref in sets[1] + (kwin, vwin, st_ref):
            ref[...] = jnp.zeros_like(ref)

    first = lax.rem(jnp.maximum(s - 1, 0), tiles_per_seq) == 0

    def body(write_set, read_set):
        pj = _ProjectStage(xa_ref, gpre_ref, lbraw_ref, w_in_ref, write_set)
        mx = _MixStage(first, xb_ref, sinks_ref, onorm_ref, gpost_ref, w_out_ref, read_set, kwin, vwin,
                       st_ref, y_ref, out_ref)
        mx.swa_window()
        pj.norm()
        mx.swa_scores(0)
        pj.hgrn_q()
        mx.swa_finish(0)
        mx.swa_scores(1)
        pj.hgrn_f()
        mx.swa_finish(1)
        mx.swa_scores(2)
        pj.hgrn_g()
        mx.swa_finish(2)
        mx.swa_scores(3)
        pj.hgrn_decay()
        mx.swa_finish(3)
        mx.swa_save_window()
        mx.hgrn_local(0)
        mx.hgrn_local(1)
        pj.swa_q()
        mx.hgrn_state(0)
        mx.hgrn_local(2)
        pj.swa_kv()
        mx.hgrn_state(1)
        mx.hgrn_local(3)
        mx.out_swa()
        mx.hgrn_state(2)
        pj.hgrn_i(0)
        mx.hgrn_state(3)
        mx.out_hgrn()
        pj.hgrn_i(1)

    @pl.when(lax.rem(s, 2) == 0)
    def _():
        body(sets[0], sets[1])

    @pl.when(lax.rem(s, 2) == 1)
    def _():
        body(sets[1], sets[0])


def _xattn_kernel(h_ref, kt_ref, v_ref, gpre_ref, gpost_ref, wq_ref, wo_ref, out_ref, o_ref):
    h = h_ref[0]
    u = (h * _rms_scale(h) * gpre_ref[...]).astype(BF16)
    q = (_dot(u, wq_ref[...]) * (XATTN_HEAD_DIM ** -0.5)).astype(BF16)
    for hd in range(XATTN_HEADS):
        cols = slice(hd * XATTN_HEAD_DIM, (hd + 1) * XATTN_HEAD_DIM)
        s = _dot(q[:, cols], kt_ref[0, cols, :])
        m = jnp.max(s, axis=-1, keepdims=True)
        p = jnp.exp(s - m)
        l = jnp.sum(p, axis=-1, keepdims=True)
        o = _dot(p.astype(BF16), v_ref[0, :, cols]) * (1.0 / l)
        o_ref[:, cols] = o.astype(BF16)
    y = _dot(o_ref[...], wo_ref[...])
    out_ref[0] = h + y * _rms_scale(y) * gpost_ref[...]


def _ffn_kernel(h_ref, gpre_ref, gpost_ref, wg_ref, wu_ref, wd_ref, out_ref):
    h = h_ref[0]
    u = (h * _rms_scale(h) * gpre_ref[...]).astype(BF16)
    gate = _dot(u, wg_ref[...])
    up = _dot(u, wu_ref[...])
    act = (gate * _sigmoid(gate) * up).astype(BF16)
    y = _dot(act, wd_ref[...])
    out_ref[0] = h + y * _rms_scale(y) * gpost_ref[...]


def _resident(shape):
    return pl.BlockSpec(shape, lambda *_: (0,) * len(shape), pipeline_mode=pl.Buffered(1))


def _params(n_axes):
    return pltpu.CompilerParams(dimension_semantics=("arbitrary",) * n_axes,
                                vmem_limit_bytes=VMEM_LIMIT_BYTES)


def kernel(x, mem, w_in, sinks, hgrn_lb, hgrn_onorm, w_out, g_mix_pre, g_mix_post, g_mem, g_x_pre,
           g_x_post, wq_x, wk_x, wv_x, wo_x, g_ffn_pre, g_ffn_post, w_gate, w_up, w_down):
    B, T, D = x.shape
    assert D == D_MODEL and T % TOKEN_TILE == 0 and mem.shape == (B, MEM_LEN, D_MODEL)
    assert w_in.shape == (1, D_MODEL, D_IN) and hgrn_lb.shape[0] == 2
    tt = TOKEN_TILE
    nt = T // tt
    n_tiles = B * nt

    def row(g):
        return g.reshape(1, -1).astype(F32)

    tok_spec = pl.BlockSpec((1, tt, D_MODEL), lambda b, t: (b, t, 0))
    tok_shape = jax.ShapeDtypeStruct((B, T, D_MODEL), F32)

    kt, v = pl.pallas_call(
        _mem_kv_kernel,
        grid=(B,),
        in_specs=[pl.BlockSpec((1, MEM_LEN, D_MODEL), lambda b: (b, 0, 0)),
                  _resident((1, D_MODEL)), _resident((D_MODEL, D_MODEL)), _resident((D_MODEL, D_MODEL))],
        out_specs=[pl.BlockSpec((1, D_MODEL, MEM_LEN), lambda b: (b, 0, 0)),
                   pl.BlockSpec((1, MEM_LEN, D_MODEL), lambda b: (b, 0, 0))],
        out_shape=[jax.ShapeDtypeStruct((B, D_MODEL, MEM_LEN), BF16),
                   jax.ShapeDtypeStruct((B, MEM_LEN, D_MODEL), BF16)],
        compiler_params=_params(1),
        name="mem_kv",
    )(mem, row(g_mem[0]), wk_x[0].astype(BF16), wv_x[0].astype(BF16))

    def project_idx(s):
        i = jnp.minimum(s, n_tiles - 1)
        return (i // nt, i % nt, 0)

    def mix_idx(s):
        i = jnp.maximum(s - 1, 0)
        return (i // nt, i % nt, 0)

    h1 = pl.pallas_call(
        functools.partial(_mixer_kernel, nt),
        grid=(n_tiles + 1,),
        in_specs=[pl.BlockSpec(memory_space=pltpu.SMEM),
                  pl.BlockSpec((1, tt, D_MODEL), project_idx),
                  pl.BlockSpec((1, tt, D_MODEL), mix_idx),
                  _resident((2, HGRN_WIDTH)), _resident((1, HGRN_HEAD_DIM)),
                  _resident((1, D_MODEL)), _resident((1, D_MODEL)),
                  _resident((D_MODEL, D_IN)), _resident((D_MODEL, D_MODEL))],
        out_specs=pl.BlockSpec((1, tt, D_MODEL), mix_idx),
        out_shape=tok_shape,
        scratch_shapes=_proj_set_shapes(tt) + _proj_set_shapes(tt) + [
            pltpu.VMEM((2, WINDOW, LANES), BF16),
            pltpu.VMEM((2, WINDOW, LANES), BF16),
            pltpu.VMEM((HGRN_HEADS, HGRN_HEAD_DIM, HGRN_HEAD_DIM), F32),
            pltpu.VMEM((tt, D_MODEL), BF16),
        ],
        compiler_params=_params(1),
        name="mixer",
    )(sinks[0].astype(F32), x, x, hgrn_lb.astype(F32), row(hgrn_onorm[0]), row(g_mix_pre[0]),
      row(g_mix_post[0]), w_in[0].astype(BF16), w_out[0].astype(BF16))

    h2 = pl.pallas_call(
        _xattn_kernel,
        grid=(B, nt),
        in_specs=[tok_spec,
                  pl.BlockSpec((1, D_MODEL, MEM_LEN), lambda b, t: (b, 0, 0)),
                  pl.BlockSpec((1, MEM_LEN, D_MODEL), lambda b, t: (b, 0, 0)),
                  _resident((1, D_MODEL)), _resident((1, D_MODEL)),
                  _resident((D_MODEL, D_MODEL)), _resident((D_MODEL, D_MODEL))],
        out_specs=tok_spec,
        out_shape=tok_shape,
        scratch_shapes=[pltpu.VMEM((tt, D_MODEL), BF16)],
        compiler_params=_params(2),
        name="xattn",
    )(h1, kt, v, row(g_x_pre[0]), row(g_x_post[0]), wq_x[0].astype(BF16), wo_x[0].astype(BF16))

    h3 = pl.pallas_call(
        _ffn_kernel,
        grid=(B, nt),
        in_specs=[tok_spec,
                  _resident((1, D_MODEL)), _resident((1, D_MODEL)),
                  _resident((D_MODEL, D_FF)), _resident((D_MODEL, D_FF)), _resident((D_FF, D_MODEL))],
        out_specs=tok_spec,
        out_shape=tok_shape,
        compiler_params=_params(2),
        name="ffn",
    )(h2, row(g_ffn_pre[0]), row(g_ffn_post[0]), w_gate[0].astype(BF16), w_up[0].astype(BF16),
      w_down[0].astype(BF16))
    return h3
```

```python
import collections
import functools

import jax
import jax.numpy as jnp
from jax import lax
from jax.experimental import pallas as pl
from jax.experimental.pallas import tpu as pltpu

D_MODEL = 1024
CHUNK = 64
SWA_HEAD_DIM = 64
SWA_HEADS = 8
SWA_KV_HEADS = 2
SWA_WIDTH = SWA_HEADS * SWA_HEAD_DIM
SWA_KV_WIDTH = SWA_KV_HEADS * SWA_HEAD_DIM
WINDOW = 128
HGRN_HEAD_DIM = 128
HGRN_WIDTH = 512
HGRN_HEADS = HGRN_WIDTH // HGRN_HEAD_DIM
D_IN = SWA_WIDTH + 2 * SWA_KV_WIDTH + 4 * HGRN_WIDTH
HGRN_COL0 = SWA_WIDTH + 2 * SWA_KV_WIDTH
MEM_LEN = 256
XATTN_HEADS = 4
XATTN_HEAD_DIM = D_MODEL // XATTN_HEADS
D_FF = 2816
RMS_EPS = 1e-6
NEG_INF = -1e30
LOG2E = 1.4426950408889634

LANES = 128
Q_BLOCK = 128
TOKEN_TILE = 512
VMEM_LIMIT_BYTES = 56 * 1024 * 1024

F32 = jnp.float32
BF16 = jnp.bfloat16


def _rms_scale(x):
    return lax.rsqrt(jnp.mean(x * x, axis=-1, keepdims=True) + RMS_EPS)


def _sigmoid(x):
    return 1.0 / (1.0 + jnp.exp2(x * (-LOG2E)))


def _dot(a, b):
    return jnp.dot(a, b, preferred_element_type=F32)


def _dot_nt(a, b):
    return lax.dot_general(a, b, (((1,), (1,)), ((), ())), preferred_element_type=F32)


def _dot_tn(a, b):
    return lax.dot_general(a, b, (((0,), (0,)), ((), ())), preferred_element_type=F32)


def _mem_kv_kernel(mem_ref, g_ref, wk_ref, wv_ref, kt_ref, v_ref):
    m = mem_ref[0]
    mn = (m * _rms_scale(m) * g_ref[...]).astype(BF16)
    k = _dot(mn, wk_ref[...])
    kt_ref[0] = jnp.transpose(k).astype(BF16)
    v_ref[0] = _dot(mn, wv_ref[...]).astype(BF16)


ProjSet = collections.namedtuple("ProjSet", "zq k v qe ke qb kd dec iv sg")


def _proj_set_shapes(tt):
    return [
        pltpu.VMEM((tt, SWA_WIDTH), BF16),
        pltpu.VMEM((2, WINDOW + tt, LANES), BF16),
        pltpu.VMEM((2, WINDOW + tt, LANES), BF16),
        pltpu.VMEM((tt, HGRN_WIDTH), BF16),
        pltpu.VMEM((tt, HGRN_WIDTH), BF16),
        pltpu.VMEM((tt, HGRN_WIDTH), BF16),
        pltpu.VMEM((tt, HGRN_WIDTH), BF16),
        pltpu.VMEM((tt // CHUNK, HGRN_WIDTH), F32),
        pltpu.VMEM((tt, HGRN_WIDTH), BF16),
        pltpu.VMEM((tt, HGRN_WIDTH), F32),
    ]


class _ProjectStage:
    def __init__(self, x_ref, gpre_ref, lbraw_ref, w_in_ref, ps):
        self.x_ref, self.gpre_ref, self.lbraw_ref, self.w_in_ref, self.ps = (
            x_ref, gpre_ref, lbraw_ref, w_in_ref, ps)
        self.tt = x_ref.shape[1]

    def _hgrn_cols(self, i):
        return _dot(self.u, self.w_in_ref[:, HGRN_COL0 + i * HGRN_WIDTH:HGRN_COL0 + (i + 1) * HGRN_WIDTH])

    def norm(self):
        x = self.x_ref[0]
        self.u = (x * _rms_scale(x) * self.gpre_ref[...]).astype(BF16)

    def swa_q(self):
        zq = _dot(self.u, self.w_in_ref[:, 0:SWA_WIDTH])
        self.ps.zq[...] = (zq * (SWA_HEAD_DIM ** -0.5 * LOG2E)).astype(BF16)

    def swa_kv(self):
        ps = self.ps
        zkv = _dot(self.u, self.w_in_ref[:, SWA_WIDTH:HGRN_COL0])
        k = zkv[:, 0:SWA_KV_WIDTH]
        v = zkv[:, SWA_KV_WIDTH:2 * SWA_KV_WIDTH]
        ps.k[0, WINDOW:, :] = k.astype(BF16)
        ps.k[1, WINDOW:, :] = pltpu.roll(k, SWA_HEAD_DIM, axis=1).astype(BF16)
        ps.v[0, WINDOW:, :] = v.astype(BF16)
        ps.v[1, WINDOW:, :] = pltpu.roll(v, SWA_HEAD_DIM, axis=1).astype(BF16)

    def hgrn_q(self):
        qh = self._hgrn_cols(0)
        self.qf = qh * _sigmoid(qh) * (HGRN_HEAD_DIM ** -0.5)

    def hgrn_f(self):
        a = self.lbraw_ref[...]
        e = jnp.exp(a - jnp.max(a, axis=0, keepdims=True))
        lb = e[0:1] / jnp.sum(e, axis=0, keepdims=True)
        f = lb + (1.0 - lb) * _sigmoid(self._hgrn_cols(1))
        self.kf = 1.0 - f
        self.log2f = jnp.log2(f)

    def hgrn_i(self, half):
        w = HGRN_WIDTH // 2
        c0 = HGRN_COL0 + 2 * HGRN_WIDTH + half * w
        self.ps.iv[:, half * w:(half + 1) * w] = _dot(self.u, self.w_in_ref[:, c0:c0 + w]).astype(BF16)

    def hgrn_g(self):
        gh = self._hgrn_cols(3)
        self.ps.sg[...] = gh * _sigmoid(gh)

    def hgrn_decay(self):
        ps = self.ps
        cs = 4 * CHUNK
        ri = lax.broadcasted_iota(jnp.int32, (cs, cs), 0)
        ci = lax.broadcasted_iota(jnp.int32, (cs, cs), 1)
        tri = jnp.where((ri // CHUNK == ci // CHUNK) & (ci <= ri), 1.0, 0.0).astype(BF16)
        for rb in range(self.tt // cs):
            lf = self.log2f[rb * cs:(rb + 1) * cs]
            hi = lf.astype(BF16)
            r1 = lf - hi.astype(F32)
            mid = r1.astype(BF16)
            lo = (r1 - mid.astype(F32)).astype(BF16)
            b_blk = _dot(tri, hi) + _dot(tri, mid) + _dot(tri, lo)
            for cc in range(cs // CHUNK):
                c = rb * (cs // CHUNK) + cc
                rows = slice(c * CHUNK, (c + 1) * CHUNK)
                b = b_blk[cc * CHUNK:(cc + 1) * CHUNK]
                b_mid = b[CHUNK // 2 - 1:CHUNK // 2]
                b_last = b[CHUNK - 1:CHUNK]
                qe = self.qf[rows] * jnp.exp2(b - b_mid)
                ke = self.kf[rows] * jnp.exp2(b_mid - b)
                ps.qe[rows, :] = qe.astype(BF16)
                ps.ke[rows, :] = ke.astype(BF16)
                ps.qb[rows, :] = (qe * jnp.exp2(b_mid)).astype(BF16)
                ps.kd[rows, :] = (ke * jnp.exp2(b_last - b_mid)).astype(BF16)
                ps.dec[c:c + 1, :] = jnp.exp2(b_last)


class _MixStage:
    PAIR_BLOCKS = SWA_WIDTH // LANES
    PAIRS_PER_KV = PAIR_BLOCKS // SWA_KV_HEADS
    N_KEYS = WINDOW + Q_BLOCK

    def __init__(self, first, x_ref, sinks_ref, onorm_ref, gpost_ref, w_out_ref, ps, kwin, vwin,
                 st_ref, y_ref, out_ref):
        self.first, self.x_ref, self.sinks_ref, self.onorm_ref, self.gpost_ref = (
            first, x_ref, sinks_ref, onorm_ref, gpost_ref)
        self.w_out_ref, self.ps, self.kwin, self.vwin, self.st_ref, self.y_ref, self.out_ref = (
            w_out_ref, ps, kwin, vwin, st_ref, y_ref, out_ref)
        self.tt = x_ref.shape[1]
        self.scores = {}
        self.amat = {}
        self.kvt = {}

    def _parity(self, ver, bi):
        return ver ^ (bi // self.PAIRS_PER_KV)

    def swa_window(self):
        ps = self.ps
        zero_win = jnp.zeros((2, WINDOW, LANES), BF16)
        ps.k[:, 0:WINDOW, :] = jnp.where(self.first, zero_win, self.kwin[...])
        ps.v[:, 0:WINDOW, :] = jnp.where(self.first, zero_win, self.vwin[...])
        lane = lax.broadcasted_iota(jnp.int32, (Q_BLOCK, LANES), 1)
        self.low_half = lane < SWA_HEAD_DIM
        row_chunk = lax.broadcasted_iota(jnp.int32, (Q_BLOCK, self.N_KEYS), 0) // CHUNK + WINDOW // CHUNK
        col = lax.broadcasted_iota(jnp.int32, (Q_BLOCK, self.N_KEYS), 1)
        col_chunk = col // CHUNK
        self.band = (col_chunk <= row_chunk) & (col_chunk >= row_chunk - WINDOW // CHUNK)
        self.band_first = self.band & ((col >= WINDOW) | jnp.logical_not(self.first))
        col_q = lax.broadcasted_iota(jnp.int32, (CHUNK, LANES), 1)
        self.keep_low = col_q < CHUNK
        self.keep_high = col_q >= CHUNK

    def _mask_scores(self, j, s):
        if j == 0:
            return jnp.where(self.band_first, s, NEG_INF)
        top = jnp.concatenate([s[0:CHUNK, 0:LANES],
                               jnp.where(self.keep_low, s[0:CHUNK, LANES:2 * LANES], NEG_INF)], axis=1)
        bot = jnp.concatenate([jnp.where(self.keep_high, s[CHUNK:2 * CHUNK, 0:LANES], NEG_INF),
                               s[CHUNK:2 * CHUNK, LANES:2 * LANES]], axis=1)
        return jnp.concatenate([top, bot], axis=0)

    def swa_scores(self, j):
        r0 = j * Q_BLOCK
        qb = self.ps.zq[r0:r0 + Q_BLOCK, :]
        for ver in range(2):
            parts = []
            for bi in range(self.PAIR_BLOCKS):
                keep = self.low_half if self._parity(ver, bi) == 0 else jnp.logical_not(self.low_half)
                parts.append(jnp.where(keep, qb[:, bi * LANES:(bi + 1) * LANES], jnp.zeros((), BF16)))
            qs = jnp.concatenate(parts, axis=0)
            self.scores[j, ver] = _dot_nt(qs, self.ps.k[ver, r0:r0 + self.N_KEYS, :])

    def swa_finish(self, j):
        r0 = j * Q_BLOCK
        halves = [[None, None] for _ in range(self.PAIR_BLOCKS)]
        for ver in range(2):
            s = self.scores.pop((j, ver))
            probs, inv_ls = [], []
            for bi in range(self.PAIR_BLOCKS):
                sink = self.sinks_ref[2 * bi + self._parity(ver, bi)] * LOG2E
                sb = self._mask_scores(j, s[bi * Q_BLOCK:(bi + 1) * Q_BLOCK])
                m = jnp.maximum(jnp.max(sb, axis=-1, keepdims=True), sink)
                p = jnp.exp2(sb - m)
                l = jnp.sum(p, axis=-1, keepdims=True) + jnp.exp2(sink - m)
                probs.append(p.astype(BF16))
                inv_ls.append(1.0 / l)
            o = _dot(jnp.concatenate(probs, axis=0), self.ps.v[ver, r0:r0 + self.N_KEYS, :])
            for bi in range(self.PAIR_BLOCKS):
                halves[bi][self._parity(ver, bi)] = o[bi * Q_BLOCK:(bi + 1) * Q_BLOCK] * inv_ls[bi]
        for bi in range(self.PAIR_BLOCKS):
            blk = jnp.where(self.low_half, halves[bi][0], halves[bi][1])
            self.y_ref[r0:r0 + Q_BLOCK, bi * LANES:(bi + 1) * LANES] = blk.astype(BF16)

    def swa_save_window(self):
        self.kwin[...] = self.ps.k[:, self.tt:self.tt + WINDOW, :]
        self.vwin[...] = self.ps.v[:, self.tt:self.tt + WINDOW, :]

    @staticmethod
    def _chunk(hd, c):
        return slice(c * CHUNK, (c + 1) * CHUNK), slice(hd * HGRN_HEAD_DIM, (hd + 1) * HGRN_HEAD_DIM)

    def hgrn_local(self, hd):
        ps = self.ps
        ri = lax.broadcasted_iota(jnp.int32, (CHUNK, CHUNK), 0)
        ci = lax.broadcasted_iota(jnp.int32, (CHUNK, CHUNK), 1)
        causal = ci <= ri
        for c in range(self.tt // CHUNK):
            rows, cols = self._chunk(hd, c)
            a = _dot_nt(ps.qe[rows, cols], ps.ke[rows, cols])
            self.amat[hd, c] = jnp.where(causal, a, 0.0).astype(BF16)
            self.kvt[hd, c] = _dot_tn(ps.iv[rows, cols], ps.kd[rows, cols])

    def hgrn_state(self, hd):
        ps = self.ps
        onorm = self.onorm_ref[...]
        st = self.st_ref[hd] * jnp.where(self.first, 0.0, 1.0)
        for c in range(self.tt // CHUNK):
            rows, cols = self._chunk(hd, c)
            o = (_dot(self.amat.pop((hd, c)), ps.iv[rows, cols])
                 + _dot_nt(ps.qb[rows, cols], st.astype(BF16)))
            st = ps.dec[c:c + 1, cols] * st + self.kvt.pop((hd, c))
            on = o * _rms_scale(o) * onorm
            self.y_ref[rows, SWA_WIDTH + hd * HGRN_HEAD_DIM:SWA_WIDTH + (hd + 1) * HGRN_HEAD_DIM] = (
                on * self.ps.sg[rows, cols]).astype(BF16)
        self.st_ref[hd] = st

    def out_swa(self):
        self.y = _dot(self.y_ref[:, 0:SWA_WIDTH], self.w_out_ref[0:SWA_WIDTH, :])

    def out_hgrn(self):
        y = self.y + _dot(self.y_ref[:, SWA_WIDTH:D_MODEL], self.w_out_ref[SWA_WIDTH:D_MODEL, :])
        self.out_ref[0] = self.x_ref[0] + y * _rms_scale(y) * self.gpost_ref[...]


def _mixer_kernel(tiles_per_seq, sinks_ref, xa_ref, xb_ref, lbraw_ref, onorm_ref, gpre_ref, gpost_ref,
                  w_in_ref, w_out_ref, out_ref, *scratch):
    n_set = len(ProjSet._fields)
    sets = (ProjSet(*scratch[0:n_set]), ProjSet(*scratch[n_set:2 * n_set]))
    kwin, vwin, st_ref, y_ref = scratch[2 * n_set:]
    s = pl.program_id(0)

    @pl.when(s == 0)
    def _():
        for ref in sets[1] + (kwin, vwin, st_ref):
            ref[...] = jnp.zeros_like(ref)

    first = lax.rem(jnp.maximum(s - 1, 0), tiles_per_seq) == 0

    def body(write_set, read_set):
        pj = _ProjectStage(xa_ref, gpre_ref, lbraw_ref, w_in_ref, write_set)
        mx = _MixStage(first, xb_ref, sinks_ref, onorm_ref, gpost_ref, w_out_ref, read_set, kwin, vwin,
                       st_ref, y_ref, out_ref)
        mx.swa_window()
        pj.norm()
        mx.swa_scores(0)
        pj.hgrn_q()
        mx.swa_finish(0)
        mx.swa_scores(1)
        pj.hgrn_f()
        mx.swa_finish(1)
        mx.swa_scores(2)
        pj.hgrn_g()
        mx.swa_finish(2)
        mx.swa_scores(3)
        pj.hgrn_decay()
        mx.swa_finish(3)
        mx.swa_save_window()
        mx.hgrn_local(0)
        mx.hgrn_local(1)
        pj.swa_q()
        mx.hgrn_state(0)
        mx.hgrn_local(2)
        pj.swa_kv()
        mx.hgrn_state(1)
        mx.hgrn_local(3)
        mx.out_swa()
        mx.hgrn_state(2)
        pj.hgrn_i(0)
        mx.hgrn_state(3)
        mx.out_hgrn()
        pj.hgrn_i(1)

    @pl.when(lax.rem(s, 2) == 0)
    def _():
        body(sets[0], sets[1])

    @pl.when(lax.rem(s, 2) == 1)
    def _():
        body(sets[1], sets[0])


def _xattn_kernel(h_ref, kt_ref, v_ref, gpre_ref, gpost_ref, wq_ref, wo_ref, out_ref, o_ref):
    half = h_ref.shape[1] // 2
    halves = (slice(0, half), slice(half, 2 * half))
    head_cols = [slice(hd * XATTN_HEAD_DIM, (hd + 1) * XATTN_HEAD_DIM) for hd in range(XATTN_HEADS)]

    def queries(rows):
        h = h_ref[0, rows, :]
        u = (h * _rms_scale(h) * gpre_ref[...]).astype(BF16)
        return (_dot(u, wq_ref[...]) * (XATTN_HEAD_DIM ** -0.5 * LOG2E)).astype(BF16)

    def scores(q):
        return [_dot(q[:, cols], kt_ref[0, cols, :]) for cols in head_cols]

    def attend(rows, s_heads):
        for cols, s in zip(head_cols, s_heads):
            p = jnp.exp2(s - jnp.max(s, axis=-1, keepdims=True))
            l = jnp.sum(p, axis=-1, keepdims=True)
            o = _dot(p.astype(BF16), v_ref[0, :, cols]) * (1.0 / l)
            o_ref[rows, cols] = o.astype(BF16)

    def project(rows):
        return _dot(o_ref[rows, :], wo_ref[...])

    def finish(rows, y):
        out_ref[0, rows, :] = h_ref[0, rows, :] + y * _rms_scale(y) * gpost_ref[...]

    q_a = queries(halves[0])
    s_a = scores(q_a)
    q_b = queries(halves[1])
    attend(halves[0], s_a)
    s_b = scores(q_b)
    y_a = project(halves[0])
    attend(halves[1], s_b)
    finish(halves[0], y_a)
    y_b = project(halves[1])
    finish(halves[1], y_b)


def _ffn_kernel(h_ref, gpre_ref, gpost_ref, wg_ref, wu_ref, wd_ref, out_ref):
    half = h_ref.shape[1] // 2
    halves = (slice(0, half), slice(half, 2 * half))

    def normed(rows):
        h = h_ref[0, rows, :]
        return (h * _rms_scale(h) * gpre_ref[...]).astype(BF16)

    def finish(rows, y):
        out_ref[0, rows, :] = h_ref[0, rows, :] + y * _rms_scale(y) * gpost_ref[...]

    u_a = normed(halves[0])
    gate_a = _dot(u_a, wg_ref[...])
    up_a = _dot(u_a, wu_ref[...])
    u_b = normed(halves[1])
    act_a = (gate_a * _sigmoid(gate_a) * up_a).astype(BF16)
    gate_b = _dot(u_b, wg_ref[...])
    y_a = _dot(act_a, wd_ref[...])
    up_b = _dot(u_b, wu_ref[...])
    finish(halves[0], y_a)
    act_b = (gate_b * _sigmoid(gate_b) * up_b).astype(BF16)
    y_b = _dot(act_b, wd_ref[...])
    finish(halves[1], y_b)


def _resident(shape):
    return pl.BlockSpec(shape, lambda *_: (0,) * len(shape), pipeline_mode=pl.Buffered(1))


def _params(n_axes):
    return pltpu.CompilerParams(dimension_semantics=("arbitrary",) * n_axes,
                                vmem_limit_bytes=VMEM_LIMIT_BYTES)


def kernel(x, mem, w_in, sinks, hgrn_lb, hgrn_onorm, w_out, g_mix_pre, g_mix_post, g_mem, g_x_pre,
           g_x_post, wq_x, wk_x, wv_x, wo_x, g_ffn_pre, g_ffn_post, w_gate, w_up, w_down):
    B, T, D = x.shape
    assert D == D_MODEL and T % TOKEN_TILE == 0 and mem.shape == (B, MEM_LEN, D_MODEL)
    assert w_in.shape == (1, D_MODEL, D_IN) and hgrn_lb.shape[0] == 2
    tt = TOKEN_TILE
    nt = T // tt
    n_tiles = B * nt

    def row(g):
        return g.reshape(1, -1).astype(F32)

    tok_spec = pl.BlockSpec((1, tt, D_MODEL), lambda b, t: (b, t, 0))
    tok_shape = jax.ShapeDtypeStruct((B, T, D_MODEL), F32)

    kt, v = pl.pallas_call(
        _mem_kv_kernel,
        grid=(B,),
        in_specs=[pl.BlockSpec((1, MEM_LEN, D_MODEL), lambda b: (b, 0, 0)),
                  _resident((1, D_MODEL)), _resident((D_MODEL, D_MODEL)), _resident((D_MODEL, D_MODEL))],
        out_specs=[pl.BlockSpec((1, D_MODEL, MEM_LEN), lambda b: (b, 0, 0)),
                   pl.BlockSpec((1, MEM_LEN, D_MODEL), lambda b: (b, 0, 0))],
        out_shape=[jax.ShapeDtypeStruct((B, D_MODEL, MEM_LEN), BF16),
                   jax.ShapeDtypeStruct((B, MEM_LEN, D_MODEL), BF16)],
        compiler_params=_params(1),
        name="mem_kv",
    )(mem, row(g_mem[0]), wk_x[0].astype(BF16), wv_x[0].astype(BF16))

    def project_idx(s):
        i = jnp.minimum(s, n_tiles - 1)
        return (i // nt, i % nt, 0)

    def mix_idx(s):
        i = jnp.maximum(s - 1, 0)
        return (i // nt, i % nt, 0)

    h1 = pl.pallas_call(
        functools.partial(_mixer_kernel, nt),
        grid=(n_tiles + 1,),
        in_specs=[pl.BlockSpec(memory_space=pltpu.SMEM),
                  pl.BlockSpec((1, tt, D_MODEL), project_idx),
                  pl.BlockSpec((1, tt, D_MODEL), mix_idx),
                  _resident((2, HGRN_WIDTH)), _resident((1, HGRN_HEAD_DIM)),
                  _resident((1, D_MODEL)), _resident((1, D_MODEL)),
                  _resident((D_MODEL, D_IN)), _resident((D_MODEL, D_MODEL))],
        out_specs=pl.BlockSpec((1, tt, D_MODEL), mix_idx),
        out_shape=tok_shape,
        scratch_shapes=_proj_set_shapes(tt) + _proj_set_shapes(tt) + [
            pltpu.VMEM((2, WINDOW, LANES), BF16),
            pltpu.VMEM((2, WINDOW, LANES), BF16),
            pltpu.VMEM((HGRN_HEADS, HGRN_HEAD_DIM, HGRN_HEAD_DIM), F32),
            pltpu.VMEM((tt, D_MODEL), BF16),
        ],
        compiler_params=_params(1),
        name="mixer",
    )(sinks[0].astype(F32), x, x, hgrn_lb.astype(F32), row(hgrn_onorm[0]), row(g_mix_pre[0]),
      row(g_mix_post[0]), w_in[0].astype(BF16), w_out[0].astype(BF16))

    h2 = pl.pallas_call(
        _xattn_kernel,
        grid=(B, nt),
        in_specs=[tok_spec,
                  pl.BlockSpec((1, D_MODEL, MEM_LEN), lambda b, t: (b, 0, 0)),
                  pl.BlockSpec((1, MEM_LEN, D_MODEL), lambda b, t: (b, 0, 0)),
                  _resident((1, D_MODEL)), _resident((1, D_MODEL)),
                  _resident((D_MODEL, D_MODEL)), _resident((D_MODEL, D_MODEL))],
        out_specs=tok_spec,
        out_shape=tok_shape,
        scratch_shapes=[pltpu.VMEM((tt, D_MODEL), BF16)],
        compiler_params=_params(2),
        name="xattn",
    )(h1, kt, v, row(g_x_pre[0]), row(g_x_post[0]), wq_x[0].astype(BF16), wo_x[0].astype(BF16))

    h3 = pl.pallas_call(
        _ffn_kernel,
        grid=(B, nt),
        in_specs=[tok_spec,
                  _resident((1, D_MODEL)), _resident((1, D_MODEL)),
                  _resident((D_MODEL, D_FF)), _resident((D_MODEL, D_FF)), _resident((D_FF, D_MODEL))],
        out_specs=tok_spec,
        out_shape=tok_shape,
        compiler_params=_params(2),
        name="ffn",
    )(h2, row(g_ffn_pre[0]), row(g_ffn_post[0]), w_gate[0].astype(BF16), w_up[0].astype(BF16),
      w_down[0].astype(BF16))
    return h3
```

```python
import collections
import functools

import jax
import jax.numpy as jnp
from jax import lax
from jax.experimental import pallas as pl
from jax.experimental.pallas import tpu as pltpu

D_MODEL = 1024
CHUNK = 64
SWA_HEAD_DIM = 64
SWA_HEADS = 8
SWA_KV_HEADS = 2
SWA_WIDTH = SWA_HEADS * SWA_HEAD_DIM
SWA_KV_WIDTH = SWA_KV_HEADS * SWA_HEAD_DIM
WINDOW = 128
HGRN_HEAD_DIM = 128
HGRN_WIDTH = 512
HGRN_HEADS = HGRN_WIDTH // HGRN_HEAD_DIM
D_IN = SWA_WIDTH + 2 * SWA_KV_WIDTH + 4 * HGRN_WIDTH
HGRN_COL0 = SWA_WIDTH + 2 * SWA_KV_WIDTH
MEM_LEN = 256
XATTN_HEADS = 4
XATTN_HEAD_DIM = D_MODEL // XATTN_HEADS
D_FF = 2816
RMS_EPS = 1e-6
NEG_INF = -1e30
LOG2E = 1.4426950408889634

LANES = 128
Q_BLOCK = 128
TOKEN_TILE = 512
VMEM_LIMIT_BYTES = 56 * 1024 * 1024

F32 = jnp.float32
BF16 = jnp.bfloat16


def _rms_scale(x):
    return lax.rsqrt(jnp.mean(x * x, axis=-1, keepdims=True) + RMS_EPS)


def _sigmoid(x):
    return 1.0 / (1.0 + jnp.exp2(x * (-LOG2E)))


def _dot(a, b):
    return jnp.dot(a, b, preferred_element_type=F32)


def _dot_nt(a, b):
    return lax.dot_general(a, b, (((1,), (1,)), ((), ())), preferred_element_type=F32)


def _dot_tn(a, b):
    return lax.dot_general(a, b, (((0,), (0,)), ((), ())), preferred_element_type=F32)


def _mem_kv_kernel(mem_ref, g_ref, wk_ref, wv_ref, kt_ref, v_ref):
    m = mem_ref[0]
    mn = (m * _rms_scale(m) * g_ref[...]).astype(BF16)
    k = _dot(mn, wk_ref[...])
    kt_ref[0] = jnp.transpose(k).astype(BF16)
    v_ref[0] = _dot(mn, wv_ref[...]).astype(BF16)


ProjSet = collections.namedtuple("ProjSet", "zq k v qe ke qb kd dec iv sg")


def _proj_set_shapes(tt):
    return [
        pltpu.VMEM((tt, SWA_WIDTH), BF16),
        pltpu.VMEM((2, WINDOW + tt, LANES), BF16),
        pltpu.VMEM((2, WINDOW + tt, LANES), BF16),
        pltpu.VMEM((tt, HGRN_WIDTH), BF16),
        pltpu.VMEM((tt, HGRN_WIDTH), BF16),
        pltpu.VMEM((tt, HGRN_WIDTH), BF16),
        pltpu.VMEM((tt, HGRN_WIDTH), BF16),
        pltpu.VMEM((tt // CHUNK, HGRN_WIDTH), F32),
        pltpu.VMEM((tt, HGRN_WIDTH), BF16),
        pltpu.VMEM((tt, HGRN_WIDTH), F32),
    ]


class _ProjectStage:
    def __init__(self, x_ref, gpre_ref, lbraw_ref, w_in_ref, ps):
        self.x_ref, self.gpre_ref, self.lbraw_ref, self.w_in_ref, self.ps = (
            x_ref, gpre_ref, lbraw_ref, w_in_ref, ps)
        self.tt = x_ref.shape[1]

    def _hgrn_cols(self, i):
        return _dot(self.u, self.w_in_ref[:, HGRN_COL0 + i * HGRN_WIDTH:HGRN_COL0 + (i + 1) * HGRN_WIDTH])

    def norm(self):
        x = self.x_ref[0]
        self.u = (x * _rms_scale(x) * self.gpre_ref[...]).astype(BF16)

    def swa_q(self):
        zq = _dot(self.u, self.w_in_ref[:, 0:SWA_WIDTH])
        self.ps.zq[...] = (zq * (SWA_HEAD_DIM ** -0.5 * LOG2E)).astype(BF16)

    def swa_kv(self):
        ps = self.ps
        zkv = _dot(self.u, self.w_in_ref[:, SWA_WIDTH:HGRN_COL0])
        k = zkv[:, 0:SWA_KV_WIDTH]
        v = zkv[:, SWA_KV_WIDTH:2 * SWA_KV_WIDTH]
        ps.k[0, WINDOW:, :] = k.astype(BF16)
        ps.k[1, WINDOW:, :] = pltpu.roll(k, SWA_HEAD_DIM, axis=1).astype(BF16)
        ps.v[0, WINDOW:, :] = v.astype(BF16)
        ps.v[1, WINDOW:, :] = pltpu.roll(v, SWA_HEAD_DIM, axis=1).astype(BF16)

    def hgrn_q(self):
        qh = self._hgrn_cols(0)
        self.qf = qh * _sigmoid(qh) * (HGRN_HEAD_DIM ** -0.5)

    def hgrn_f(self):
        a = self.lbraw_ref[...]
        e = jnp.exp(a - jnp.max(a, axis=0, keepdims=True))
        lb = e[0:1] / jnp.sum(e, axis=0, keepdims=True)
        f = lb + (1.0 - lb) * _sigmoid(self._hgrn_cols(1))
        self.kf = 1.0 - f
        self.log2f = jnp.log2(f)

    def hgrn_i(self, half):
        w = HGRN_WIDTH // 2
        c0 = HGRN_COL0 + 2 * HGRN_WIDTH + half * w
        self.ps.iv[:, half * w:(half + 1) * w] = _dot(self.u, self.w_in_ref[:, c0:c0 + w]).astype(BF16)

    def hgrn_g(self):
        gh = self._hgrn_cols(3)
        self.ps.sg[...] = gh * _sigmoid(gh)

    def hgrn_decay(self):
        ps = self.ps
        cs = 4 * CHUNK
        ri = lax.broadcasted_iota(jnp.int32, (cs, cs), 0)
        ci = lax.broadcasted_iota(jnp.int32, (cs, cs), 1)
        tri = jnp.where((ri // CHUNK == ci // CHUNK) & (ci <= ri), 1.0, 0.0).astype(BF16)
        for rb in range(self.tt // cs):
            lf = self.log2f[rb * cs:(rb + 1) * cs]
            hi = lf.astype(BF16)
            r1 = lf - hi.astype(F32)
            mid = r1.astype(BF16)
            lo = (r1 - mid.astype(F32)).astype(BF16)
            b_blk = _dot(tri, hi) + _dot(tri, mid) + _dot(tri, lo)
            for cc in range(cs // CHUNK):
                c = rb * (cs // CHUNK) + cc
                rows = slice(c * CHUNK, (c + 1) * CHUNK)
                b = b_blk[cc * CHUNK:(cc + 1) * CHUNK]
                b_mid = b[CHUNK // 2 - 1:CHUNK // 2]
                b_last = b[CHUNK - 1:CHUNK]
                qe = self.qf[rows] * jnp.exp2(b - b_mid)
                ke = self.kf[rows] * jnp.exp2(b_mid - b)
                ps.qe[rows, :] = qe.astype(BF16)
                ps.ke[rows, :] = ke.astype(BF16)
                ps.qb[rows, :] = (qe * jnp.exp2(b_mid)).astype(BF16)
                ps.kd[rows, :] = (ke * jnp.exp2(b_last - b_mid)).astype(BF16)
                ps.dec[c:c + 1, :] = jnp.exp2(b_last)


class _MixStage:
    PAIR_BLOCKS = SWA_WIDTH // LANES
    PAIRS_PER_KV = PAIR_BLOCKS // SWA_KV_HEADS
    N_KEYS = WINDOW + Q_BLOCK

    def __init__(self, first, x_ref, sinks_ref, onorm_ref, gpost_ref, w_out_ref, ps, kwin, vwin,
                 st_ref, y_ref, out_ref):
        self.first, self.x_ref, self.sinks_ref, self.onorm_ref, self.gpost_ref = (
            first, x_ref, sinks_ref, onorm_ref, gpost_ref)
        self.w_out_ref, self.ps, self.kwin, self.vwin, self.st_ref, self.y_ref, self.out_ref = (
            w_out_ref, ps, kwin, vwin, st_ref, y_ref, out_ref)
        self.tt = x_ref.shape[1]
        self.scores = {}
        self.amat = {}
        self.kvt = {}

    def _parity(self, ver, bi):
        return ver ^ (bi // self.PAIRS_PER_KV)

    def swa_window(self):
        ps = self.ps
        zero_win = jnp.zeros((2, WINDOW, LANES), BF16)
        ps.k[:, 0:WINDOW, :] = jnp.where(self.first, zero_win, self.kwin[...])
        ps.v[:, 0:WINDOW, :] = jnp.where(self.first, zero_win, self.vwin[...])
        lane = lax.broadcasted_iota(jnp.int32, (Q_BLOCK, LANES), 1)
        self.low_half = lane < SWA_HEAD_DIM
        row_chunk = lax.broadcasted_iota(jnp.int32, (Q_BLOCK, self.N_KEYS), 0) // CHUNK + WINDOW // CHUNK
        col = lax.broadcasted_iota(jnp.int32, (Q_BLOCK, self.N_KEYS), 1)
        col_chunk = col // CHUNK
        self.band = (col_chunk <= row_chunk) & (col_chunk >= row_chunk - WINDOW // CHUNK)
        self.band_first = self.band & ((col >= WINDOW) | jnp.logical_not(self.first))
        col_q = lax.broadcasted_iota(jnp.int32, (CHUNK, LANES), 1)
        self.keep_low = col_q < CHUNK
        self.keep_high = col_q >= CHUNK

    def _mask_scores(self, j, s):
        if j == 0:
            return jnp.where(self.band_first, s, NEG_INF)
        top = jnp.concatenate([s[0:CHUNK, 0:LANES],
                               jnp.where(self.keep_low, s[0:CHUNK, LANES:2 * LANES], NEG_INF)], axis=1)
        bot = jnp.concatenate([jnp.where(self.keep_high, s[CHUNK:2 * CHUNK, 0:LANES], NEG_INF),
                               s[CHUNK:2 * CHUNK, LANES:2 * LANES]], axis=1)
        return jnp.concatenate([top, bot], axis=0)

    def swa_scores(self, j):
        r0 = j * Q_BLOCK
        qb = self.ps.zq[r0:r0 + Q_BLOCK, :]
        for ver in range(2):
            parts = []
            for bi in range(self.PAIR_BLOCKS):
                keep = self.low_half if self._parity(ver, bi) == 0 else jnp.logical_not(self.low_half)
                parts.append(jnp.where(keep, qb[:, bi * LANES:(bi + 1) * LANES], jnp.zeros((), BF16)))
            qs = jnp.concatenate(parts, axis=0)
            self.scores[j, ver] = _dot_nt(qs, self.ps.k[ver, r0:r0 + self.N_KEYS, :])

    def swa_finish(self, j):
        r0 = j * Q_BLOCK
        halves = [[None, None] for _ in range(self.PAIR_BLOCKS)]
        for ver in range(2):
            s = self.scores.pop((j, ver))
            probs, inv_ls = [], []
            for bi in range(self.PAIR_BLOCKS):
                sink = self.sinks_ref[2 * bi + self._parity(ver, bi)] * LOG2E
                sb = self._mask_scores(j, s[bi * Q_BLOCK:(bi + 1) * Q_BLOCK])
                m = jnp.maximum(jnp.max(sb, axis=-1, keepdims=True), sink)
                p = jnp.exp2(sb - m)
                l = jnp.sum(p, axis=-1, keepdims=True) + jnp.exp2(sink - m)
                probs.append(p.astype(BF16))
                inv_ls.append(1.0 / l)
            o = _dot(jnp.concatenate(probs, axis=0), self.ps.v[ver, r0:r0 + self.N_KEYS, :])
            for bi in range(self.PAIR_BLOCKS):
                halves[bi][self._parity(ver, bi)] = o[bi * Q_BLOCK:(bi + 1) * Q_BLOCK] * inv_ls[bi]
        for bi in range(self.PAIR_BLOCKS):
            blk = jnp.where(self.low_half, halves[bi][0], halves[bi][1])
            self.y_ref[r0:r0 + Q_BLOCK, bi * LANES:(bi + 1) * LANES] = blk.astype(BF16)

    def swa_save_window(self):
        self.kwin[...] = self.ps.k[:, self.tt:self.tt + WINDOW, :]
        self.vwin[...] = self.ps.v[:, self.tt:self.tt + WINDOW, :]

    @staticmethod
    def _chunk(hd, c):
        return slice(c * CHUNK, (c + 1) * CHUNK), slice(hd * HGRN_HEAD_DIM, (hd + 1) * HGRN_HEAD_DIM)

    def hgrn_local(self, hd):
        ps = self.ps
        ri = lax.broadcasted_iota(jnp.int32, (CHUNK, CHUNK), 0)
        ci = lax.broadcasted_iota(jnp.int32, (CHUNK, CHUNK), 1)
        causal = ci <= ri
        for c in range(self.tt // CHUNK):
            rows, cols = self._chunk(hd, c)
            a = _dot_nt(ps.qe[rows, cols], ps.ke[rows, cols])
            self.amat[hd, c] = jnp.where(causal, a, 0.0).astype(BF16)
            self.kvt[hd, c] = _dot_tn(ps.iv[rows, cols], ps.kd[rows, cols])

    def hgrn_state(self, hd):
        ps = self.ps
        onorm = self.onorm_ref[...]
        st = self.st_ref[hd] * jnp.where(self.first, 0.0, 1.0)
        for c in range(self.tt // CHUNK):
            rows, cols = self._chunk(hd, c)
            o = (_dot(self.amat.pop((hd, c)), ps.iv[rows, cols])
                 + _dot_nt(ps.qb[rows, cols], st.astype(BF16)))
            st = ps.dec[c:c + 1, cols] * st + self.kvt.pop((hd, c))
            on = o * _rms_scale(o) * onorm
            self.y_ref[rows, SWA_WIDTH + hd * HGRN_HEAD_DIM:SWA_WIDTH + (hd + 1) * HGRN_HEAD_DIM] = (
                on * self.ps.sg[rows, cols]).astype(BF16)
        self.st_ref[hd] = st

    def out_swa(self):
        self.y = _dot(self.y_ref[:, 0:SWA_WIDTH], self.w_out_ref[0:SWA_WIDTH, :])

    def out_hgrn(self):
        y = self.y + _dot(self.y_ref[:, SWA_WIDTH:D_MODEL], self.w_out_ref[SWA_WIDTH:D_MODEL, :])
        self.out_ref[0] = self.x_ref[0] + y * _rms_scale(y) * self.gpost_ref[...]


def _mixer_kernel(tiles_per_seq, sinks_ref, xa_ref, xb_ref, lbraw_ref, onorm_ref, gpre_ref, gpost_ref,
                  w_in_ref, w_out_ref, out_ref, *scratch):
    n_set = len(ProjSet._fields)
    sets = (ProjSet(*scratch[0:n_set]), ProjSet(*scratch[n_set:2 * n_set]))
    kwin, vwin, st_ref, y_ref = scratch[2 * n_set:]
    s = pl.program_id(0)

    @pl.when(s == 0)
    def _():
        for ref in sets[1] + (kwin, vwin, st_ref):
            ref[...] = jnp.zeros_like(ref)

    first = lax.rem(jnp.maximum(s - 1, 0), tiles_per_seq) == 0

    def body(write_set, read_set):
        pj = _ProjectStage(xa_ref, gpre_ref, lbraw_ref, w_in_ref, write_set)
        mx = _MixStage(first, xb_ref, sinks_ref, onorm_ref, gpost_ref, w_out_ref, read_set, kwin, vwin,
                       st_ref, y_ref, out_ref)
        mx.swa_window()
        pj.norm()
        mx.swa_scores(0)
        pj.hgrn_q()
        mx.swa_finish(0)
        mx.swa_scores(1)
        pj.hgrn_f()
        mx.swa_finish(1)
        mx.swa_scores(2)
        pj.hgrn_g()
        mx.swa_finish(2)
        mx.swa_scores(3)
        pj.hgrn_decay()
        mx.swa_finish(3)
        mx.swa_save_window()
        mx.hgrn_local(0)
        mx.hgrn_local(1)
        pj.swa_q()
        mx.hgrn_state(0)
        mx.hgrn_local(2)
        pj.swa_kv()
        mx.hgrn_state(1)
        mx.hgrn_local(3)
        mx.out_swa()
        mx.hgrn_state(2)
        pj.hgrn_i(0)
        mx.hgrn_state(3)
        mx.out_hgrn()
        pj.hgrn_i(1)

    @pl.when(lax.rem(s, 2) == 0)
    def _():
        body(sets[0], sets[1])

    @pl.when(lax.rem(s, 2) == 1)
    def _():
        body(sets[1], sets[0])


def _xattn_ffn_kernel(h_ref, kt_ref, v_ref, gxpre_ref, gxpost_ref, wq_ref, wo_ref,
                      gfpre_ref, gfpost_ref, wg_ref, wu_ref, wd_ref, out_ref, o_ref):
    half = h_ref.shape[1] // 2
    halves = (slice(0, half), slice(half, 2 * half))
    head_cols = [slice(hd * XATTN_HEAD_DIM, (hd + 1) * XATTN_HEAD_DIM) for hd in range(XATTN_HEADS)]

    def queries(rows):
        h = h_ref[0, rows, :]
        u = (h * _rms_scale(h) * gxpre_ref[...]).astype(BF16)
        return (_dot(u, wq_ref[...]) * (XATTN_HEAD_DIM ** -0.5 * LOG2E)).astype(BF16)

    def scores(q):
        return [_dot(q[:, cols], kt_ref[0, cols, :]) for cols in head_cols]

    def attend(rows, s_heads):
        for cols, s in zip(head_cols, s_heads):
            p = jnp.exp2(s - jnp.max(s, axis=-1, keepdims=True))
            l = jnp.sum(p, axis=-1, keepdims=True)
            o = _dot(p.astype(BF16), v_ref[0, :, cols]) * (1.0 / l)
            o_ref[rows, cols] = o.astype(BF16)

    def attn_out(rows):
        return _dot(o_ref[rows, :], wo_ref[...])

    def attn_residual(rows, y):
        return h_ref[0, rows, :] + y * _rms_scale(y) * gxpost_ref[...]

    def ffn_in(h2):
        return (h2 * _rms_scale(h2) * gfpre_ref[...]).astype(BF16)

    def ffn_residual(rows, h2, y):
        out_ref[0, rows, :] = h2 + y * _rms_scale(y) * gfpost_ref[...]

    q_a = queries(halves[0])
    s_a = scores(q_a)
    q_b = queries(halves[1])
    attend(halves[0], s_a)
    s_b = scores(q_b)
    y_a = attn_out(halves[0])
    attend(halves[1], s_b)
    h2_a = attn_residual(halves[0], y_a)
    y_b = attn_out(halves[1])
    u_a = ffn_in(h2_a)
    gate_a = _dot(u_a, wg_ref[...])
    up_a = _dot(u_a, wu_ref[...])
    h2_b = attn_residual(halves[1], y_b)
    u_b = ffn_in(h2_b)
    act_a = (gate_a * _sigmoid(gate_a) * up_a).astype(BF16)
    gate_b = _dot(u_b, wg_ref[...])
    z_a = _dot(act_a, wd_ref[...])
    up_b = _dot(u_b, wu_ref[...])
    ffn_residual(halves[0], h2_a, z_a)
    act_b = (gate_b * _sigmoid(gate_b) * up_b).astype(BF16)
    z_b = _dot(act_b, wd_ref[...])
    ffn_residual(halves[1], h2_b, z_b)


def _resident(shape):
    return pl.BlockSpec(shape, lambda *_: (0,) * len(shape), pipeline_mode=pl.Buffered(1))


def _params(n_axes):
    return pltpu.CompilerParams(dimension_semantics=("arbitrary",) * n_axes,
                                vmem_limit_bytes=VMEM_LIMIT_BYTES)


def kernel(x, mem, w_in, sinks, hgrn_lb, hgrn_onorm, w_out, g_mix_pre, g_mix_post, g_mem, g_x_pre,
           g_x_post, wq_x, wk_x, wv_x, wo_x, g_ffn_pre, g_ffn_post, w_gate, w_up, w_down):
    B, T, D = x.shape
    assert D == D_MODEL and T % TOKEN_TILE == 0 and mem.shape == (B, MEM_LEN, D_MODEL)
    assert w_in.shape == (1, D_MODEL, D_IN) and hgrn_lb.shape[0] == 2
    tt = TOKEN_TILE
    nt = T // tt
    n_tiles = B * nt

    def row(g):
        return g.reshape(1, -1).astype(F32)

    tok_spec = pl.BlockSpec((1, tt, D_MODEL), lambda b, t: (b, t, 0))
    tok_shape = jax.ShapeDtypeStruct((B, T, D_MODEL), F32)

    kt, v = pl.pallas_call(
        _mem_kv_kernel,
        grid=(B,),
        in_specs=[pl.BlockSpec((1, MEM_LEN, D_MODEL), lambda b: (b, 0, 0)),
                  _resident((1, D_MODEL)), _resident((D_MODEL, D_MODEL)), _resident((D_MODEL, D_MODEL))],
        out_specs=[pl.BlockSpec((1, D_MODEL, MEM_LEN), lambda b: (b, 0, 0)),
                   pl.BlockSpec((1, MEM_LEN, D_MODEL), lambda b: (b, 0, 0))],
        out_shape=[jax.ShapeDtypeStruct((B, D_MODEL, MEM_LEN), BF16),
                   jax.ShapeDtypeStruct((B, MEM_LEN, D_MODEL), BF16)],
        compiler_params=_params(1),
        name="mem_kv",
    )(mem, row(g_mem[0]), wk_x[0].astype(BF16), wv_x[0].astype(BF16))

    def project_idx(s):
        i = jnp.minimum(s, n_tiles - 1)
        return (i // nt, i % nt, 0)

    def mix_idx(s):
        i = jnp.maximum(s - 1, 0)
        return (i // nt, i % nt, 0)

    h1 = pl.pallas_call(
        functools.partial(_mixer_kernel, nt),
        grid=(n_tiles + 1,),
        in_specs=[pl.BlockSpec(memory_space=pltpu.SMEM),
                  pl.BlockSpec((1, tt, D_MODEL), project_idx),
                  pl.BlockSpec((1, tt, D_MODEL), mix_idx),
                  _resident((2, HGRN_WIDTH)), _resident((1, HGRN_HEAD_DIM)),
                  _resident((1, D_MODEL)), _resident((1, D_MODEL)),
                  _resident((D_MODEL, D_IN)), _resident((D_MODEL, D_MODEL))],
        out_specs=pl.BlockSpec((1, tt, D_MODEL), mix_idx),
        out_shape=tok_shape,
        scratch_shapes=_proj_set_shapes(tt) + _proj_set_shapes(tt) + [
            pltpu.VMEM((2, WINDOW, LANES), BF16),
            pltpu.VMEM((2, WINDOW, LANES), BF16),
            pltpu.VMEM((HGRN_HEADS, HGRN_HEAD_DIM, HGRN_HEAD_DIM), F32),
            pltpu.VMEM((tt, D_MODEL), BF16),
        ],
        compiler_params=_params(1),
        name="mixer",
    )(sinks[0].astype(F32), x, x, hgrn_lb.astype(F32), row(hgrn_onorm[0]), row(g_mix_pre[0]),
      row(g_mix_post[0]), w_in[0].astype(BF16), w_out[0].astype(BF16))

    h3 = pl.pallas_call(
        _xattn_ffn_kernel,
        grid=(B, nt),
        in_specs=[tok_spec,
                  pl.BlockSpec((1, D_MODEL, MEM_LEN), lambda b, t: (b, 0, 0)),
                  pl.BlockSpec((1, MEM_LEN, D_MODEL), lambda b, t: (b, 0, 0)),
                  _resident((1, D_MODEL)), _resident((1, D_MODEL)),
                  _resident((D_MODEL, D_MODEL)), _resident((D_MODEL, D_MODEL)),
                  _resident((1, D_MODEL)), _resident((1, D_MODEL)),
                  _resident((D_MODEL, D_FF)), _resident((D_MODEL, D_FF)), _resident((D_FF, D_MODEL))],
        out_specs=tok_spec,
        out_shape=tok_shape,
        scratch_shapes=[pltpu.VMEM((tt, D_MODEL), BF16)],
        compiler_params=_params(2),
        name="xattn_ffn",
    )(h1, kt, v, row(g_x_pre[0]), row(g_x_post[0]), wq_x[0].astype(BF16), wo_x[0].astype(BF16),
      row(g_ffn_pre[0]), row(g_ffn_post[0]), w_gate[0].astype(BF16), w_up[0].astype(BF16),
      w_down[0].astype(BF16))
    return h3
```

```python
import collections
import functools

import jax
import jax.numpy as jnp
from jax import lax
from jax.experimental import pallas as pl
from jax.experimental.pallas import tpu as pltpu

D_MODEL = 1024
CHUNK = 64
SWA_HEAD_DIM = 64
SWA_HEADS = 8
SWA_KV_HEADS = 2
SWA_WIDTH = SWA_HEADS * SWA_HEAD_DIM
SWA_KV_WIDTH = SWA_KV_HEADS * SWA_HEAD_DIM
WINDOW = 128
HGRN_HEAD_DIM = 128
HGRN_WIDTH = 512
HGRN_HEADS = HGRN_WIDTH // HGRN_HEAD_DIM
D_IN = SWA_WIDTH + 2 * SWA_KV_WIDTH + 4 * HGRN_WIDTH
HGRN_COL0 = SWA_WIDTH + 2 * SWA_KV_WIDTH
MEM_LEN = 256
XATTN_HEADS = 4
XATTN_HEAD_DIM = D_MODEL // XATTN_HEADS
D_FF = 2816
RMS_EPS = 1e-6
NEG_INF = -1e30
LOG2E = 1.4426950408889634

LANES = 128
Q_BLOCK = 128
TOKEN_TILE = 512
VMEM_LIMIT_BYTES = 56 * 1024 * 1024

F32 = jnp.float32
BF16 = jnp.bfloat16


def _rms_scale(x):
    return lax.rsqrt(jnp.mean(x * x, axis=-1, keepdims=True) + RMS_EPS)


def _sigmoid(x):
    return 1.0 / (1.0 + jnp.exp2(x * (-LOG2E)))


def _dot(a, b):
    return jnp.dot(a, b, preferred_element_type=F32)


def _dot_nt(a, b):
    return lax.dot_general(a, b, (((1,), (1,)), ((), ())), preferred_element_type=F32)


def _dot_tn(a, b):
    return lax.dot_general(a, b, (((0,), (0,)), ((), ())), preferred_element_type=F32)


def _cast_slabs(src_refs, dst_refs):
    for src, dst in zip(src_refs, dst_refs):
        dst[...] = src[...].astype(BF16)


def _mem_kv_kernel(mem_ref, g_ref, wk_ref, wv_ref, w_in_ref, w_out_ref,
                   kt_ref, v_ref, w_in_bf_ref, w_out_bf_ref, wk_bf, wv_bf):
    @pl.when(pl.program_id(0) == 0)
    def _():
        _cast_slabs((wk_ref, wv_ref), (wk_bf, wv_bf))

    _cast_slabs((w_in_ref, w_out_ref), (w_in_bf_ref, w_out_bf_ref))
    m = mem_ref[0]
    mn = (m * _rms_scale(m) * g_ref[...]).astype(BF16)
    k = _dot(mn, wk_bf[...])
    kt_ref[0] = jnp.transpose(k).astype(BF16)
    v_ref[0] = _dot(mn, wv_bf[...]).astype(BF16)


ProjSet = collections.namedtuple("ProjSet", "zq k v qe ke qb kd dec iv sg")


def _proj_set_shapes(tt):
    return [
        pltpu.VMEM((tt, SWA_WIDTH), BF16),
        pltpu.VMEM((2, WINDOW + tt, LANES), BF16),
        pltpu.VMEM((2, WINDOW + tt, LANES), BF16),
        pltpu.VMEM((tt, HGRN_WIDTH), BF16),
        pltpu.VMEM((tt, HGRN_WIDTH), BF16),
        pltpu.VMEM((tt, HGRN_WIDTH), BF16),
        pltpu.VMEM((tt, HGRN_WIDTH), BF16),
        pltpu.VMEM((tt // CHUNK, HGRN_WIDTH), F32),
        pltpu.VMEM((tt, HGRN_WIDTH), BF16),
        pltpu.VMEM((tt, HGRN_WIDTH), F32),
    ]


class _ProjectStage:
    def __init__(self, x_ref, gpre_ref, lbraw_ref, w_in_ref, ps):
        self.x_ref, self.gpre_ref, self.lbraw_ref, self.w_in_ref, self.ps = (
            x_ref, gpre_ref, lbraw_ref, w_in_ref, ps)
        self.tt = x_ref.shape[1]

    def _hgrn_cols(self, i):
        return _dot(self.u, self.w_in_ref[:, HGRN_COL0 + i * HGRN_WIDTH:HGRN_COL0 + (i + 1) * HGRN_WIDTH])

    def norm(self):
        x = self.x_ref[0]
        self.u = (x * _rms_scale(x) * self.gpre_ref[...]).astype(BF16)

    def swa_q(self):
        zq = _dot(self.u, self.w_in_ref[:, 0:SWA_WIDTH])
        self.ps.zq[...] = (zq * (SWA_HEAD_DIM ** -0.5 * LOG2E)).astype(BF16)

    def swa_kv(self):
        ps = self.ps
        zkv = _dot(self.u, self.w_in_ref[:, SWA_WIDTH:HGRN_COL0])
        k = zkv[:, 0:SWA_KV_WIDTH]
        v = zkv[:, SWA_KV_WIDTH:2 * SWA_KV_WIDTH]
        ps.k[0, WINDOW:, :] = k.astype(BF16)
        ps.k[1, WINDOW:, :] = pltpu.roll(k, SWA_HEAD_DIM, axis=1).astype(BF16)
        ps.v[0, WINDOW:, :] = v.astype(BF16)
        ps.v[1, WINDOW:, :] = pltpu.roll(v, SWA_HEAD_DIM, axis=1).astype(BF16)

    def hgrn_q(self):
        qh = self._hgrn_cols(0)
        self.qf = qh * _sigmoid(qh) * (HGRN_HEAD_DIM ** -0.5)

    def hgrn_f(self):
        a = self.lbraw_ref[...]
        e = jnp.exp(a - jnp.max(a, axis=0, keepdims=True))
        lb = e[0:1] / jnp.sum(e, axis=0, keepdims=True)
        f = lb + (1.0 - lb) * _sigmoid(self._hgrn_cols(1))
        self.kf = 1.0 - f
        self.log2f = jnp.log2(f)

    def hgrn_i(self, half):
        w = HGRN_WIDTH // 2
        c0 = HGRN_COL0 + 2 * HGRN_WIDTH + half * w
        self.ps.iv[:, half * w:(half + 1) * w] = _dot(self.u, self.w_in_ref[:, c0:c0 + w]).astype(BF16)

    def hgrn_g(self):
        gh = self._hgrn_cols(3)
        self.ps.sg[...] = gh * _sigmoid(gh)

    def hgrn_decay(self):
        ps = self.ps
        cs = 4 * CHUNK
        ri = lax.broadcasted_iota(jnp.int32, (cs, cs), 0)
        ci = lax.broadcasted_iota(jnp.int32, (cs, cs), 1)
        tri = jnp.where((ri // CHUNK == ci // CHUNK) & (ci <= ri), 1.0, 0.0).astype(BF16)
        for rb in range(self.tt // cs):
            lf = self.log2f[rb * cs:(rb + 1) * cs]
            hi = lf.astype(BF16)
            r1 = lf - hi.astype(F32)
            mid = r1.astype(BF16)
            lo = (r1 - mid.astype(F32)).astype(BF16)
            b_blk = _dot(tri, hi) + _dot(tri, mid) + _dot(tri, lo)
            for cc in range(cs // CHUNK):
                c = rb * (cs // CHUNK) + cc
                rows = slice(c * CHUNK, (c + 1) * CHUNK)
                b = b_blk[cc * CHUNK:(cc + 1) * CHUNK]
                b_mid = b[CHUNK // 2 - 1:CHUNK // 2]
                b_last = b[CHUNK - 1:CHUNK]
                qe = self.qf[rows] * jnp.exp2(b - b_mid)
                ke = self.kf[rows] * jnp.exp2(b_mid - b)
                ps.qe[rows, :] = qe.astype(BF16)
                ps.ke[rows, :] = ke.astype(BF16)
                ps.qb[rows, :] = (qe * jnp.exp2(b_mid)).astype(BF16)
                ps.kd[rows, :] = (ke * jnp.exp2(b_last - b_mid)).astype(BF16)
                ps.dec[c:c + 1, :] = jnp.exp2(b_last)


class _MixStage:
    PAIR_BLOCKS = SWA_WIDTH // LANES
    PAIRS_PER_KV = PAIR_BLOCKS // SWA_KV_HEADS
    N_KEYS = WINDOW + Q_BLOCK

    def __init__(self, first, x_ref, sinks_ref, onorm_ref, gpost_ref, w_out_ref, ps, kwin, vwin,
                 st_ref, y_ref, out_ref):
        self.first, self.x_ref, self.sinks_ref, self.onorm_ref, self.gpost_ref = (
            first, x_ref, sinks_ref, onorm_ref, gpost_ref)
        self.w_out_ref, self.ps, self.kwin, self.vwin, self.st_ref, self.y_ref, self.out_ref = (
            w_out_ref, ps, kwin, vwin, st_ref, y_ref, out_ref)
        self.tt = x_ref.shape[1]
        self.scores = {}
        self.amat = {}
        self.kvt = {}

    def _parity(self, ver, bi):
        return ver ^ (bi // self.PAIRS_PER_KV)

    def swa_window(self):
        ps = self.ps
        zero_win = jnp.zeros((2, WINDOW, LANES), BF16)
        ps.k[:, 0:WINDOW, :] = jnp.where(self.first, zero_win, self.kwin[...])
        ps.v[:, 0:WINDOW, :] = jnp.where(self.first, zero_win, self.vwin[...])
        lane = lax.broadcasted_iota(jnp.int32, (Q_BLOCK, LANES), 1)
        self.low_half = lane < SWA_HEAD_DIM
        row_chunk = lax.broadcasted_iota(jnp.int32, (Q_BLOCK, self.N_KEYS), 0) // CHUNK + WINDOW // CHUNK
        col = lax.broadcasted_iota(jnp.int32, (Q_BLOCK, self.N_KEYS), 1)
        col_chunk = col // CHUNK
        self.band = (col_chunk <= row_chunk) & (col_chunk >= row_chunk - WINDOW // CHUNK)
        self.band_first = self.band & ((col >= WINDOW) | jnp.logical_not(self.first))
        col_q = lax.broadcasted_iota(jnp.int32, (CHUNK, LANES), 1)
        self.keep_low = col_q < CHUNK
        self.keep_high = col_q >= CHUNK

    def _mask_scores(self, j, s):
        if j == 0:
            return jnp.where(self.band_first, s, NEG_INF)
        top = jnp.concatenate([s[0:CHUNK, 0:LANES],
                               jnp.where(self.keep_low, s[0:CHUNK, LANES:2 * LANES], NEG_INF)], axis=1)
        bot = jnp.concatenate([jnp.where(self.keep_high, s[CHUNK:2 * CHUNK, 0:LANES], NEG_INF),
                               s[CHUNK:2 * CHUNK, LANES:2 * LANES]], axis=1)
        return jnp.concatenate([top, bot], axis=0)

    def swa_scores(self, j):
        r0 = j * Q_BLOCK
        qb = self.ps.zq[r0:r0 + Q_BLOCK, :]
        for ver in range(2):
            parts = []
            for bi in range(self.PAIR_BLOCKS):
                keep = self.low_half if self._parity(ver, bi) == 0 else jnp.logical_not(self.low_half)
                parts.append(jnp.where(keep, qb[:, bi * LANES:(bi + 1) * LANES], jnp.zeros((), BF16)))
            qs = jnp.concatenate(parts, axis=0)
            self.scores[j, ver] = _dot_nt(qs, self.ps.k[ver, r0:r0 + self.N_KEYS, :])

    def swa_finish(self, j):
        r0 = j * Q_BLOCK
        halves = [[None, None] for _ in range(self.PAIR_BLOCKS)]
        for ver in range(2):
            s = self.scores.pop((j, ver))
            probs, inv_ls = [], []
            for bi in range(self.PAIR_BLOCKS):
                sink = self.sinks_ref[2 * bi + self._parity(ver, bi)] * LOG2E
                sb = self._mask_scores(j, s[bi * Q_BLOCK:(bi + 1) * Q_BLOCK])
                m = jnp.maximum(jnp.max(sb, axis=-1, keepdims=True), sink)
                p = jnp.exp2(sb - m)
                l = jnp.sum(p, axis=-1, keepdims=True) + jnp.exp2(sink - m)
                probs.append(p.astype(BF16))
                inv_ls.append(1.0 / l)
            o = _dot(jnp.concatenate(probs, axis=0), self.ps.v[ver, r0:r0 + self.N_KEYS, :])
            for bi in range(self.PAIR_BLOCKS):
                halves[bi][self._parity(ver, bi)] = o[bi * Q_BLOCK:(bi + 1) * Q_BLOCK] * inv_ls[bi]
        for bi in range(self.PAIR_BLOCKS):
            blk = jnp.where(self.low_half, halves[bi][0], halves[bi][1])
            self.y_ref[r0:r0 + Q_BLOCK, bi * LANES:(bi + 1) * LANES] = blk.astype(BF16)

    def swa_save_window(self):
        self.kwin[...] = self.ps.k[:, self.tt:self.tt + WINDOW, :]
        self.vwin[...] = self.ps.v[:, self.tt:self.tt + WINDOW, :]

    @staticmethod
    def _chunk(hd, c):
        return slice(c * CHUNK, (c + 1) * CHUNK), slice(hd * HGRN_HEAD_DIM, (hd + 1) * HGRN_HEAD_DIM)

    def hgrn_local(self, hd):
        ps = self.ps
        ri = lax.broadcasted_iota(jnp.int32, (CHUNK, CHUNK), 0)
        ci = lax.broadcasted_iota(jnp.int32, (CHUNK, CHUNK), 1)
        causal = ci <= ri
        for c in range(self.tt // CHUNK):
            rows, cols = self._chunk(hd, c)
            a = _dot_nt(ps.qe[rows, cols], ps.ke[rows, cols])
            self.amat[hd, c] = jnp.where(causal, a, 0.0).astype(BF16)
            self.kvt[hd, c] = _dot_tn(ps.iv[rows, cols], ps.kd[rows, cols])

    def hgrn_state(self, hd):
        ps = self.ps
        onorm = self.onorm_ref[...]
        st = self.st_ref[hd] * jnp.where(self.first, 0.0, 1.0)
        for c in range(self.tt // CHUNK):
            rows, cols = self._chunk(hd, c)
            o = (_dot(self.amat.pop((hd, c)), ps.iv[rows, cols])
                 + _dot_nt(ps.qb[rows, cols], st.astype(BF16)))
            st = ps.dec[c:c + 1, cols] * st + self.kvt.pop((hd, c))
            on = o * _rms_scale(o) * onorm
            self.y_ref[rows, SWA_WIDTH + hd * HGRN_HEAD_DIM:SWA_WIDTH + (hd + 1) * HGRN_HEAD_DIM] = (
                on * self.ps.sg[rows, cols]).astype(BF16)
        self.st_ref[hd] = st

    def out_swa(self):
        self.y = _dot(self.y_ref[:, 0:SWA_WIDTH], self.w_out_ref[0:SWA_WIDTH, :])

    def out_hgrn(self):
        y = self.y + _dot(self.y_ref[:, SWA_WIDTH:D_MODEL], self.w_out_ref[SWA_WIDTH:D_MODEL, :])
        self.out_ref[0] = self.x_ref[0] + y * _rms_scale(y) * self.gpost_ref[...]


def _mixer_kernel(tiles_per_seq, n_down_slabs, sinks_ref, xa_ref, xb_ref, lbraw_ref, onorm_ref, gpre_ref,
                  gpost_ref, w_in_ref, w_out_ref, wq_ref, wo_ref, wg_ref, wu_ref, wd_ref,
                  out_ref, wq_bf_ref, wo_bf_ref, wg_bf_ref, wu_bf_ref, wd_bf_ref, *scratch):
    n_set = len(ProjSet._fields)
    sets = (ProjSet(*scratch[0:n_set]), ProjSet(*scratch[n_set:2 * n_set]))
    kwin, vwin, st_ref, y_ref = scratch[2 * n_set:]
    s = pl.program_id(0)

    _cast_slabs((wq_ref, wo_ref, wg_ref, wu_ref), (wq_bf_ref, wo_bf_ref, wg_bf_ref, wu_bf_ref))

    @pl.when(s < n_down_slabs)
    def _():
        _cast_slabs((wd_ref,), (wd_bf_ref,))

    @pl.when(s == 0)
    def _():
        for ref in sets[1] + (kwin, vwin, st_ref):
            ref[...] = jnp.zeros_like(ref)

    first = lax.rem(jnp.maximum(s - 1, 0), tiles_per_seq) == 0

    def body(write_set, read_set):
        pj = _ProjectStage(xa_ref, gpre_ref, lbraw_ref, w_in_ref, write_set)
        mx = _MixStage(first, xb_ref, sinks_ref, onorm_ref, gpost_ref, w_out_ref, read_set, kwin, vwin,
                       st_ref, y_ref, out_ref)
        mx.swa_window()
        pj.norm()
        mx.swa_scores(0)
        pj.hgrn_q()
        mx.swa_finish(0)
        mx.swa_scores(1)
        pj.hgrn_f()
        mx.swa_finish(1)
        mx.swa_scores(2)
        pj.hgrn_g()
        mx.swa_finish(2)
        mx.swa_scores(3)
        pj.hgrn_decay()
        mx.swa_finish(3)
        mx.swa_save_window()
        mx.hgrn_local(0)
        mx.hgrn_local(1)
        pj.swa_q()
        mx.hgrn_state(0)
        mx.hgrn_local(2)
        pj.swa_kv()
        mx.hgrn_state(1)
        mx.hgrn_local(3)
        mx.out_swa()
        mx.hgrn_state(2)
        pj.hgrn_i(0)
        mx.hgrn_state(3)
        mx.out_hgrn()
        pj.hgrn_i(1)

    @pl.when(lax.rem(s, 2) == 0)
    def _():
        body(sets[0], sets[1])

    @pl.when(lax.rem(s, 2) == 1)
    def _():
        body(sets[1], sets[0])


def _xattn_ffn_kernel(h_ref, kt_ref, v_ref, gxpre_ref, gxpost_ref, wq_ref, wo_ref,
                      gfpre_ref, gfpost_ref, wg_ref, wu_ref, wd_ref, out_ref, o_ref):
    half = h_ref.shape[1] // 2
    halves = (slice(0, half), slice(half, 2 * half))
    head_cols = [slice(hd * XATTN_HEAD_DIM, (hd + 1) * XATTN_HEAD_DIM) for hd in range(XATTN_HEADS)]

    def queries(rows):
        h = h_ref[0, rows, :]
        u = (h * _rms_scale(h) * gxpre_ref[...]).astype(BF16)
        return (_dot(u, wq_ref[...]) * (XATTN_HEAD_DIM ** -0.5 * LOG2E)).astype(BF16)

    def scores(q):
        return [_dot(q[:, cols], kt_ref[0, cols, :]) for cols in head_cols]

    def attend(rows, s_heads):
        for cols, s in zip(head_cols, s_heads):
            p = jnp.exp2(s - jnp.max(s, axis=-1, keepdims=True))
            l = jnp.sum(p, axis=-1, keepdims=True)
            o = _dot(p.astype(BF16), v_ref[0, :, cols]) * (1.0 / l)
            o_ref[rows, cols] = o.astype(BF16)

    def attn_out(rows):
        return _dot(o_ref[rows, :], wo_ref[...])

    def attn_residual(rows, y):
        return h_ref[0, rows, :] + y * _rms_scale(y) * gxpost_ref[...]

    def ffn_in(h2):
        return (h2 * _rms_scale(h2) * gfpre_ref[...]).astype(BF16)

    def ffn_residual(rows, h2, y):
        out_ref[0, rows, :] = h2 + y * _rms_scale(y) * gfpost_ref[...]

    q_a = queries(halves[0])
    s_a = scores(q_a)
    q_b = queries(halves[1])
    attend(halves[0], s_a)
    s_b = scores(q_b)
    y_a = attn_out(halves[0])
    attend(halves[1], s_b)
    h2_a = attn_residual(halves[0], y_a)
    y_b = attn_out(halves[1])
    u_a = ffn_in(h2_a)
    gate_a = _dot(u_a, wg_ref[...])
    up_a = _dot(u_a, wu_ref[...])
    h2_b = attn_residual(halves[1], y_b)
    u_b = ffn_in(h2_b)
    act_a = (gate_a * _sigmoid(gate_a) * up_a).astype(BF16)
    gate_b = _dot(u_b, wg_ref[...])
    z_a = _dot(act_a, wd_ref[...])
    up_b = _dot(u_b, wu_ref[...])
    ffn_residual(halves[0], h2_a, z_a)
    act_b = (gate_b * _sigmoid(gate_b) * up_b).astype(BF16)
    z_b = _dot(act_b, wd_ref[...])
    ffn_residual(halves[1], h2_b, z_b)


def _resident(shape):
    return pl.BlockSpec(shape, lambda *_: (0,) * len(shape), pipeline_mode=pl.Buffered(1))


def _params(n_axes):
    return pltpu.CompilerParams(dimension_semantics=("arbitrary",) * n_axes,
                                vmem_limit_bytes=VMEM_LIMIT_BYTES)


def kernel(x, mem, w_in, sinks, hgrn_lb, hgrn_onorm, w_out, g_mix_pre, g_mix_post, g_mem, g_x_pre,
           g_x_post, wq_x, wk_x, wv_x, wo_x, g_ffn_pre, g_ffn_post, w_gate, w_up, w_down):
    B, T, D = x.shape
    assert D == D_MODEL and T % TOKEN_TILE == 0 and mem.shape == (B, MEM_LEN, D_MODEL)
    assert w_in.shape == (1, D_MODEL, D_IN) and hgrn_lb.shape[0] == 2
    tt = TOKEN_TILE
    nt = T // tt
    n_tiles = B * nt

    def row(g):
        return g.reshape(1, -1).astype(F32)

    tok_spec = pl.BlockSpec((1, tt, D_MODEL), lambda b, t: (b, t, 0))
    tok_shape = jax.ShapeDtypeStruct((B, T, D_MODEL), F32)

    def slab_spec(rows, cols, n_slabs):
        return pl.BlockSpec((rows, cols), lambda s: (jnp.minimum(s, n_slabs - 1), 0))

    def bf16_like(w):
        return jax.ShapeDtypeStruct(w.shape, BF16)

    w_in_f, w_out_f = w_in[0].astype(F32), w_out[0].astype(F32)
    in_rows = D_MODEL // B
    kt, v, w_in_bf, w_out_bf = pl.pallas_call(
        _mem_kv_kernel,
        grid=(B,),
        in_specs=[pl.BlockSpec((1, MEM_LEN, D_MODEL), lambda b: (b, 0, 0)),
                  _resident((1, D_MODEL)), _resident((D_MODEL, D_MODEL)), _resident((D_MODEL, D_MODEL)),
                  slab_spec(in_rows, D_IN, B), slab_spec(in_rows, D_MODEL, B)],
        out_specs=[pl.BlockSpec((1, D_MODEL, MEM_LEN), lambda b: (b, 0, 0)),
                   pl.BlockSpec((1, MEM_LEN, D_MODEL), lambda b: (b, 0, 0)),
                   slab_spec(in_rows, D_IN, B), slab_spec(in_rows, D_MODEL, B)],
        out_shape=[jax.ShapeDtypeStruct((B, D_MODEL, MEM_LEN), BF16),
                   jax.ShapeDtypeStruct((B, MEM_LEN, D_MODEL), BF16),
                   bf16_like(w_in_f), bf16_like(w_out_f)],
        scratch_shapes=[pltpu.VMEM((D_MODEL, D_MODEL), BF16), pltpu.VMEM((D_MODEL, D_MODEL), BF16)],
        compiler_params=_params(1),
        name="mem_kv",
    )(mem, row(g_mem[0]), wk_x[0].astype(F32), wv_x[0].astype(F32), w_in_f, w_out_f)

    def project_idx(s):
        i = jnp.minimum(s, n_tiles - 1)
        return (i // nt, i % nt, 0)

    def mix_idx(s):
        i = jnp.maximum(s - 1, 0)
        return (i // nt, i % nt, 0)

    next_w = [w[0].astype(F32) for w in (wq_x, wo_x, w_gate, w_up, w_down)]
    step_rows = D_MODEL // n_tiles
    down_rows = LANES
    n_down_slabs = D_FF // down_rows
    assert step_rows * n_tiles == D_MODEL and step_rows % 16 == 0 and n_down_slabs <= n_tiles
    next_w_specs = [slab_spec(step_rows, D_MODEL, n_tiles), slab_spec(step_rows, D_MODEL, n_tiles),
                    slab_spec(step_rows, D_FF, n_tiles), slab_spec(step_rows, D_FF, n_tiles),
                    slab_spec(down_rows, D_MODEL, n_down_slabs)]

    h1, wq_bf, wo_bf, wg_bf, wu_bf, wd_bf = pl.pallas_call(
        functools.partial(_mixer_kernel, nt, n_down_slabs),
        grid=(n_tiles + 1,),
        in_specs=[pl.BlockSpec(memory_space=pltpu.SMEM),
                  pl.BlockSpec((1, tt, D_MODEL), project_idx),
                  pl.BlockSpec((1, tt, D_MODEL), mix_idx),
                  _resident((2, HGRN_WIDTH)), _resident((1, HGRN_HEAD_DIM)),
                  _resident((1, D_MODEL)), _resident((1, D_MODEL)),
                  _resident((D_MODEL, D_IN)), _resident((D_MODEL, D_MODEL))] + next_w_specs,
        out_specs=[pl.BlockSpec((1, tt, D_MODEL), mix_idx)] + next_w_specs,
        out_shape=[tok_shape] + [bf16_like(w) for w in next_w],
        scratch_shapes=_proj_set_shapes(tt) + _proj_set_shapes(tt) + [
            pltpu.VMEM((2, WINDOW, LANES), BF16),
            pltpu.VMEM((2, WINDOW, LANES), BF16),
            pltpu.VMEM((HGRN_HEADS, HGRN_HEAD_DIM, HGRN_HEAD_DIM), F32),
            pltpu.VMEM((tt, D_MODEL), BF16),
        ],
        compiler_params=_params(1),
        name="mixer",
    )(sinks[0].astype(F32), x, x, hgrn_lb.astype(F32), row(hgrn_onorm[0]), row(g_mix_pre[0]),
      row(g_mix_post[0]), w_in_bf, w_out_bf, *next_w)

    h3 = pl.pallas_call(
        _xattn_ffn_kernel,
        grid=(B, nt),
        in_specs=[tok_spec,
                  pl.BlockSpec((1, D_MODEL, MEM_LEN), lambda b, t: (b, 0, 0)),
                  pl.BlockSpec((1, MEM_LEN, D_MODEL), lambda b, t: (b, 0, 0)),
                  _resident((1, D_MODEL)), _resident((1, D_MODEL)),
                  _resident((D_MODEL, D_MODEL)), _resident((D_MODEL, D_MODEL)),
                  _resident((1, D_MODEL)), _resident((1, D_MODEL)),
                  _resident((D_MODEL, D_FF)), _resident((D_MODEL, D_FF)), _resident((D_FF, D_MODEL))],
        out_specs=tok_spec,
        out_shape=tok_shape,
        scratch_shapes=[pltpu.VMEM((tt, D_MODEL), BF16)],
        compiler_params=_params(2),
        name="xattn_ffn",
    )(h1, kt, v, row(g_x_pre[0]), row(g_x_post[0]), wq_bf, wo_bf,
      row(g_ffn_pre[0]), row(g_ffn_post[0]), wg_bf, wu_bf, wd_bf)
    return h3
```

```python
import collections
import functools

import jax
import jax.numpy as jnp
from jax import lax
from jax.experimental import pallas as pl
from jax.experimental.pallas import tpu as pltpu

D_MODEL = 1024
CHUNK = 64
SWA_HEAD_DIM = 64
SWA_HEADS = 8
SWA_KV_HEADS = 2
SWA_WIDTH = SWA_HEADS * SWA_HEAD_DIM
SWA_KV_WIDTH = SWA_KV_HEADS * SWA_HEAD_DIM
WINDOW = 128
HGRN_HEAD_DIM = 128
HGRN_WIDTH = 512
HGRN_HEADS = HGRN_WIDTH // HGRN_HEAD_DIM
D_IN = SWA_WIDTH + 2 * SWA_KV_WIDTH + 4 * HGRN_WIDTH
HGRN_COL0 = SWA_WIDTH + 2 * SWA_KV_WIDTH
MEM_LEN = 256
XATTN_HEADS = 4
XATTN_HEAD_DIM = D_MODEL // XATTN_HEADS
D_FF = 2816
RMS_EPS = 1e-6
NEG_INF = -1e30
LOG2E = 1.4426950408889634

LANES = 128
Q_BLOCK = 128
TOKEN_TILE = 512
VMEM_LIMIT_BYTES = 56 * 1024 * 1024

F32 = jnp.float32
BF16 = jnp.bfloat16


def _rms_scale(x):
    return lax.rsqrt(jnp.mean(x * x, axis=-1, keepdims=True) + RMS_EPS)


def _sigmoid(x):
    return 1.0 / (1.0 + jnp.exp2(x * (-LOG2E)))


def _dot(a, b):
    return jnp.dot(a, b, preferred_element_type=F32)


def _dot_nt(a, b):
    return lax.dot_general(a, b, (((1,), (1,)), ((), ())), preferred_element_type=F32)


def _dot_tn(a, b):
    return lax.dot_general(a, b, (((0,), (0,)), ((), ())), preferred_element_type=F32)


def _cast_slabs(src_refs, dst_refs):
    for src, dst in zip(src_refs, dst_refs):
        dst[...] = src[...].astype(BF16)


def _mem_kv_kernel(mem_ref, g_ref, wk_ref, wv_ref, w_in_ref, w_out_ref,
                   kt_ref, v_ref, w_in_bf_ref, w_out_bf_ref, wk_bf, wv_bf):
    @pl.when(pl.program_id(0) == 0)
    def _():
        _cast_slabs((wk_ref, wv_ref), (wk_bf, wv_bf))

    _cast_slabs((w_in_ref, w_out_ref), (w_in_bf_ref, w_out_bf_ref))
    m = mem_ref[0]
    mn = (m * _rms_scale(m) * g_ref[...]).astype(BF16)
    k = _dot(mn, wk_bf[...])
    kt_ref[0] = jnp.transpose(k).astype(BF16)
    v_ref[0] = _dot(mn, wv_bf[...]).astype(BF16)


ProjSet = collections.namedtuple("ProjSet", "zq k v qe ke qb kd dec iv sg")


def _proj_set_shapes(tt):
    return [
        pltpu.VMEM((tt, SWA_WIDTH), BF16),
        pltpu.VMEM((2, WINDOW + tt, LANES), BF16),
        pltpu.VMEM((2, WINDOW + tt, LANES), BF16),
        pltpu.VMEM((tt, HGRN_WIDTH), BF16),
        pltpu.VMEM((tt, HGRN_WIDTH), BF16),
        pltpu.VMEM((tt, HGRN_WIDTH), BF16),
        pltpu.VMEM((tt, HGRN_WIDTH), BF16),
        pltpu.VMEM((tt // CHUNK, HGRN_WIDTH, HGRN_HEAD_DIM), F32),
        pltpu.VMEM((tt, HGRN_WIDTH), BF16),
        pltpu.VMEM((tt, HGRN_WIDTH), F32),
    ]


class _ProjectStage:
    def __init__(self, x_ref, gpre_ref, lbraw_ref, w_in_ref, ps):
        self.x_ref, self.gpre_ref, self.lbraw_ref, self.w_in_ref, self.ps = (
            x_ref, gpre_ref, lbraw_ref, w_in_ref, ps)
        self.tt = x_ref.shape[1]

    def _hgrn_cols(self, i):
        return _dot(self.u, self.w_in_ref[:, HGRN_COL0 + i * HGRN_WIDTH:HGRN_COL0 + (i + 1) * HGRN_WIDTH])

    def norm(self):
        x = self.x_ref[0]
        self.u = (x * _rms_scale(x) * self.gpre_ref[...]).astype(BF16)

    def swa_q(self):
        zq = _dot(self.u, self.w_in_ref[:, 0:SWA_WIDTH])
        self.ps.zq[...] = (zq * (SWA_HEAD_DIM ** -0.5 * LOG2E)).astype(BF16)

    def swa_kv(self):
        ps = self.ps
        zkv = _dot(self.u, self.w_in_ref[:, SWA_WIDTH:HGRN_COL0])
        k = zkv[:, 0:SWA_KV_WIDTH]
        v = zkv[:, SWA_KV_WIDTH:2 * SWA_KV_WIDTH]
        ps.k[0, WINDOW:, :] = k.astype(BF16)
        ps.k[1, WINDOW:, :] = pltpu.roll(k, SWA_HEAD_DIM, axis=1).astype(BF16)
        ps.v[0, WINDOW:, :] = v.astype(BF16)
        ps.v[1, WINDOW:, :] = pltpu.roll(v, SWA_HEAD_DIM, axis=1).astype(BF16)

    def hgrn_q(self):
        qh = self._hgrn_cols(0)
        self.qf = qh * _sigmoid(qh) * (HGRN_HEAD_DIM ** -0.5)

    def hgrn_f(self):
        a = self.lbraw_ref[...]
        e = jnp.exp(a - jnp.max(a, axis=0, keepdims=True))
        lb = e[0:1] / jnp.sum(e, axis=0, keepdims=True)
        f = lb + (1.0 - lb) * _sigmoid(self._hgrn_cols(1))
        self.kf = 1.0 - f
        self.log2f = jnp.log2(f)

    def hgrn_i(self, half):
        w = HGRN_WIDTH // 2
        c0 = HGRN_COL0 + 2 * HGRN_WIDTH + half * w
        self.ps.iv[:, half * w:(half + 1) * w] = _dot(self.u, self.w_in_ref[:, c0:c0 + w]).astype(BF16)

    def hgrn_g(self):
        gh = self._hgrn_cols(3)
        self.ps.sg[...] = gh * _sigmoid(gh)

    def hgrn_decay(self):
        ps = self.ps
        cs = 4 * CHUNK
        ri = lax.broadcasted_iota(jnp.int32, (cs, cs), 0)
        ci = lax.broadcasted_iota(jnp.int32, (cs, cs), 1)
        tri = jnp.where((ri // CHUNK == ci // CHUNK) & (ci <= ri), 1.0, 0.0).astype(BF16)
        for rb in range(self.tt // cs):
            lf = self.log2f[rb * cs:(rb + 1) * cs]
            hi = lf.astype(BF16)
            r1 = lf - hi.astype(F32)
            mid = r1.astype(BF16)
            lo = (r1 - mid.astype(F32)).astype(BF16)
            b_blk = _dot(tri, hi) + _dot(tri, mid) + _dot(tri, lo)
            for cc in range(cs // CHUNK):
                c = rb * (cs // CHUNK) + cc
                rows = slice(c * CHUNK, (c + 1) * CHUNK)
                b = b_blk[cc * CHUNK:(cc + 1) * CHUNK]
                b_mid = b[CHUNK // 2 - 1:CHUNK // 2]
                b_last = b[CHUNK - 1:CHUNK]
                qe = self.qf[rows] * jnp.exp2(b - b_mid)
                ke = self.kf[rows] * jnp.exp2(b_mid - b)
                ps.qe[rows, :] = qe.astype(BF16)
                ps.ke[rows, :] = ke.astype(BF16)
                ps.qb[rows, :] = (qe * jnp.exp2(b_mid)).astype(BF16)
                ps.kd[rows, :] = (ke * jnp.exp2(b_last - b_mid)).astype(BF16)
                dec_rows = jnp.broadcast_to(jnp.exp2(b_last), (HGRN_HEAD_DIM, HGRN_WIDTH))
                ps.dec[c] = jnp.transpose(dec_rows)


class _MixStage:
    PAIR_BLOCKS = SWA_WIDTH // LANES
    PAIRS_PER_KV = PAIR_BLOCKS // SWA_KV_HEADS
    N_KEYS = WINDOW + Q_BLOCK

    def __init__(self, first, x_ref, sinks_ref, onorm_ref, gpost_ref, w_out_ref, ps, kwin, vwin,
                 st_ref, y_ref, out_ref):
        self.first, self.x_ref, self.sinks_ref, self.onorm_ref, self.gpost_ref = (
            first, x_ref, sinks_ref, onorm_ref, gpost_ref)
        self.w_out_ref, self.ps, self.kwin, self.vwin, self.st_ref, self.y_ref, self.out_ref = (
            w_out_ref, ps, kwin, vwin, st_ref, y_ref, out_ref)
        self.tt = x_ref.shape[1]
        self.scores = {}
        self.amat = {}
        self.kv = {}

    def _parity(self, ver, bi):
        return ver ^ (bi // self.PAIRS_PER_KV)

    def swa_window(self):
        ps = self.ps
        zero_win = jnp.zeros((2, WINDOW, LANES), BF16)
        ps.k[:, 0:WINDOW, :] = jnp.where(self.first, zero_win, self.kwin[...])
        ps.v[:, 0:WINDOW, :] = jnp.where(self.first, zero_win, self.vwin[...])
        lane = lax.broadcasted_iota(jnp.int32, (Q_BLOCK, LANES), 1)
        self.low_half = lane < SWA_HEAD_DIM
        row_chunk = lax.broadcasted_iota(jnp.int32, (Q_BLOCK, self.N_KEYS), 0) // CHUNK + WINDOW // CHUNK
        col = lax.broadcasted_iota(jnp.int32, (Q_BLOCK, self.N_KEYS), 1)
        col_chunk = col // CHUNK
        self.band = (col_chunk <= row_chunk) & (col_chunk >= row_chunk - WINDOW // CHUNK)
        self.band_first = self.band & ((col >= WINDOW) | jnp.logical_not(self.first))
        col_q = lax.broadcasted_iota(jnp.int32, (CHUNK, LANES), 1)
        self.keep_low = col_q < CHUNK
        self.keep_high = col_q >= CHUNK

    def _mask_scores(self, j, s):
        if j == 0:
            return jnp.where(self.band_first, s, NEG_INF)
        top = jnp.concatenate([s[0:CHUNK, 0:LANES],
                               jnp.where(self.keep_low, s[0:CHUNK, LANES:2 * LANES], NEG_INF)], axis=1)
        bot = jnp.concatenate([jnp.where(self.keep_high, s[CHUNK:2 * CHUNK, 0:LANES], NEG_INF),
                               s[CHUNK:2 * CHUNK, LANES:2 * LANES]], axis=1)
        return jnp.concatenate([top, bot], axis=0)

    def swa_scores(self, j):
        r0 = j * Q_BLOCK
        qb = self.ps.zq[r0:r0 + Q_BLOCK, :]
        for ver in range(2):
            parts = []
            for bi in range(self.PAIR_BLOCKS):
                keep = self.low_half if self._parity(ver, bi) == 0 else jnp.logical_not(self.low_half)
                parts.append(jnp.where(keep, qb[:, bi * LANES:(bi + 1) * LANES], jnp.zeros((), BF16)))
            qs = jnp.concatenate(parts, axis=0)
            self.scores[j, ver] = _dot_nt(qs, self.ps.k[ver, r0:r0 + self.N_KEYS, :])

    def swa_finish(self, j):
        r0 = j * Q_BLOCK
        halves = [[None, None] for _ in range(self.PAIR_BLOCKS)]
        for ver in range(2):
            s = self.scores.pop((j, ver))
            probs, inv_ls = [], []
            for bi in range(self.PAIR_BLOCKS):
                sink = self.sinks_ref[2 * bi + self._parity(ver, bi)] * LOG2E
                sb = self._mask_scores(j, s[bi * Q_BLOCK:(bi + 1) * Q_BLOCK])
                m = jnp.maximum(jnp.max(sb, axis=-1, keepdims=True), sink)
                p = jnp.exp2(sb - m)
                l = jnp.sum(p, axis=-1, keepdims=True) + jnp.exp2(sink - m)
                probs.append(p.astype(BF16))
                inv_ls.append(1.0 / l)
            o = _dot(jnp.concatenate(probs, axis=0), self.ps.v[ver, r0:r0 + self.N_KEYS, :])
            for bi in range(self.PAIR_BLOCKS):
                halves[bi][self._parity(ver, bi)] = o[bi * Q_BLOCK:(bi + 1) * Q_BLOCK] * inv_ls[bi]
        for bi in range(self.PAIR_BLOCKS):
            blk = jnp.where(self.low_half, halves[bi][0], halves[bi][1])
            self.y_ref[r0:r0 + Q_BLOCK, bi * LANES:(bi + 1) * LANES] = blk.astype(BF16)

    def swa_save_window(self):
        self.kwin[...] = self.ps.k[:, self.tt:self.tt + WINDOW, :]
        self.vwin[...] = self.ps.v[:, self.tt:self.tt + WINDOW, :]

    @staticmethod
    def _chunk(hd, c):
        return slice(c * CHUNK, (c + 1) * CHUNK), slice(hd * HGRN_HEAD_DIM, (hd + 1) * HGRN_HEAD_DIM)

    def hgrn_local(self, pair):
        ps = self.ps
        ri = lax.broadcasted_iota(jnp.int32, (CHUNK, CHUNK), 0)
        ci = lax.broadcasted_iota(jnp.int32, (CHUNK, CHUNK), 1)
        causal = ci <= ri
        zero = jnp.zeros((CHUNK, HGRN_HEAD_DIM), BF16)
        hd_a, hd_b = 2 * pair, 2 * pair + 1
        for c in range(self.tt // CHUNK):
            rows, cols_a = self._chunk(hd_a, c)
            _, cols_b = self._chunk(hd_b, c)
            for hd, cols in ((hd_a, cols_a), (hd_b, cols_b)):
                a = _dot_nt(ps.qe[rows, cols], ps.ke[rows, cols])
                self.amat[hd, c] = jnp.where(causal, a, 0.0).astype(BF16)
            kd_ab = jnp.concatenate([ps.kd[rows, cols_a], ps.kd[rows, cols_b]], axis=0)
            iv_ab = jnp.concatenate([jnp.concatenate([ps.iv[rows, cols_a], zero], axis=1),
                                     jnp.concatenate([zero, ps.iv[rows, cols_b]], axis=1)], axis=0)
            kv = _dot_tn(kd_ab, iv_ab)
            self.kv[hd_a, c] = kv[:, 0:HGRN_HEAD_DIM]
            self.kv[hd_b, c] = kv[:, HGRN_HEAD_DIM:2 * HGRN_HEAD_DIM]

    def hgrn_state(self, hd):
        ps = self.ps
        onorm = self.onorm_ref[...]
        st = self.st_ref[hd] * jnp.where(self.first, 0.0, 1.0)
        for c in range(self.tt // CHUNK):
            rows, cols = self._chunk(hd, c)
            lhs = jnp.concatenate([ps.qb[rows, cols], self.amat.pop((hd, c))], axis=1)
            rhs = jnp.concatenate([st.astype(BF16), ps.iv[rows, cols]], axis=0)
            o = _dot(lhs, rhs)
            st = ps.dec[c, cols, :] * st + self.kv.pop((hd, c))
            on = o * _rms_scale(o) * onorm
            self.y_ref[rows, SWA_WIDTH + hd * HGRN_HEAD_DIM:SWA_WIDTH + (hd + 1) * HGRN_HEAD_DIM] = (
                on * self.ps.sg[rows, cols]).astype(BF16)
        self.st_ref[hd] = st

    def out_swa(self):
        self.y = _dot(self.y_ref[:, 0:SWA_WIDTH], self.w_out_ref[0:SWA_WIDTH, :])

    def out_hgrn(self):
        y = self.y + _dot(self.y_ref[:, SWA_WIDTH:D_MODEL], self.w_out_ref[SWA_WIDTH:D_MODEL, :])
        self.out_ref[0] = self.x_ref[0] + y * _rms_scale(y) * self.gpost_ref[...]


def _mixer_kernel(tiles_per_seq, n_down_slabs, sinks_ref, xa_ref, xb_ref, lbraw_ref, onorm_ref, gpre_ref,
                  gpost_ref, w_in_ref, w_out_ref, wq_ref, wo_ref, wg_ref, wu_ref, wd_ref,
                  out_ref, wq_bf_ref, wo_bf_ref, wg_bf_ref, wu_bf_ref, wd_bf_ref, *scratch):
    n_set = len(ProjSet._fields)
    sets = (ProjSet(*scratch[0:n_set]), ProjSet(*scratch[n_set:2 * n_set]))
    kwin, vwin, st_ref, y_ref = scratch[2 * n_set:]
    s = pl.program_id(0)

    _cast_slabs((wq_ref, wo_ref, wg_ref, wu_ref), (wq_bf_ref, wo_bf_ref, wg_bf_ref, wu_bf_ref))

    @pl.when(s < n_down_slabs)
    def _():
        _cast_slabs((wd_ref,), (wd_bf_ref,))

    @pl.when(s == 0)
    def _():
        for ref in sets[1] + (kwin, vwin, st_ref):
            ref[...] = jnp.zeros_like(ref)

    first = lax.rem(jnp.maximum(s - 1, 0), tiles_per_seq) == 0

    def body(write_set, read_set):
        pj = _ProjectStage(xa_ref, gpre_ref, lbraw_ref, w_in_ref, write_set)
        mx = _MixStage(first, xb_ref, sinks_ref, onorm_ref, gpost_ref, w_out_ref, read_set, kwin, vwin,
                       st_ref, y_ref, out_ref)
        mx.swa_window()
        pj.norm()
        mx.swa_scores(0)
        pj.hgrn_q()
        mx.swa_finish(0)
        mx.swa_scores(1)
        pj.hgrn_f()
        mx.swa_finish(1)
        mx.swa_scores(2)
        pj.hgrn_g()
        mx.swa_finish(2)
        mx.swa_scores(3)
        pj.hgrn_decay()
        mx.swa_finish(3)
        mx.swa_save_window()
        mx.hgrn_local(0)
        pj.swa_q()
        mx.hgrn_state(0)
        mx.hgrn_state(1)
        mx.hgrn_local(1)
        pj.swa_kv()
        mx.out_swa()
        mx.hgrn_state(2)
        pj.hgrn_i(0)
        mx.hgrn_state(3)
        mx.out_hgrn()
        pj.hgrn_i(1)

    @pl.when(lax.rem(s, 2) == 0)
    def _():
        body(sets[0], sets[1])

    @pl.when(lax.rem(s, 2) == 1)
    def _():
        body(sets[1], sets[0])


def _xattn_ffn_kernel(h_ref, kt_ref, v_ref, gxpre_ref, gxpost_ref, wq_ref, wo_ref,
                      gfpre_ref, gfpost_ref, wg_ref, wu_ref, wd_ref, out_ref, o_ref):
    half = h_ref.shape[1] // 2
    halves = (slice(0, half), slice(half, 2 * half))
    head_cols = [slice(hd * XATTN_HEAD_DIM, (hd + 1) * XATTN_HEAD_DIM) for hd in range(XATTN_HEADS)]

    def queries(rows):
        h = h_ref[0, rows, :]
        u = (h * _rms_scale(h) * gxpre_ref[...]).astype(BF16)
        return (_dot(u, wq_ref[...]) * (XATTN_HEAD_DIM ** -0.5 * LOG2E)).astype(BF16)

    def scores(q):
        return [_dot(q[:, cols], kt_ref[0, cols, :]) for cols in head_cols]

    def attend(rows, s_heads):
        for cols, s in zip(head_cols, s_heads):
            p = jnp.exp2(s - jnp.max(s, axis=-1, keepdims=True))
            l = jnp.sum(p, axis=-1, keepdims=True)
            o = _dot(p.astype(BF16), v_ref[0, :, cols]) * (1.0 / l)
            o_ref[rows, cols] = o.astype(BF16)

    def attn_out(rows):
        return _dot(o_ref[rows, :], wo_ref[...])

    def attn_residual(rows, y):
        return h_ref[0, rows, :] + y * _rms_scale(y) * gxpost_ref[...]

    def ffn_in(h2):
        return (h2 * _rms_scale(h2) * gfpre_ref[...]).astype(BF16)

    def ffn_residual(rows, h2, y):
        out_ref[0, rows, :] = h2 + y * _rms_scale(y) * gfpost_ref[...]

    q_a = queries(halves[0])
    s_a = scores(q_a)
    q_b = queries(halves[1])
    attend(halves[0], s_a)
    s_b = scores(q_b)
    y_a = attn_out(halves[0])
    attend(halves[1], s_b)
    h2_a = attn_residual(halves[0], y_a)
    y_b = attn_out(halves[1])
    u_a = ffn_in(h2_a)
    gate_a = _dot(u_a, wg_ref[...])
    up_a = _dot(u_a, wu_ref[...])
    h2_b = attn_residual(halves[1], y_b)
    u_b = ffn_in(h2_b)
    act_a = (gate_a * _sigmoid(gate_a) * up_a).astype(BF16)
    gate_b = _dot(u_b, wg_ref[...])
    z_a = _dot(act_a, wd_ref[...])
    up_b = _dot(u_b, wu_ref[...])
    ffn_residual(halves[0], h2_a, z_a)
    act_b = (gate_b * _sigmoid(gate_b) * up_b).astype(BF16)
    z_b = _dot(act_b, wd_ref[...])
    ffn_residual(halves[1], h2_b, z_b)


def _resident(shape):
    return pl.BlockSpec(shape, lambda *_: (0,) * len(shape), pipeline_mode=pl.Buffered(1))


def _params(n_axes):
    return pltpu.CompilerParams(dimension_semantics=("arbitrary",) * n_axes,
                                vmem_limit_bytes=VMEM_LIMIT_BYTES)


def kernel(x, mem, w_in, sinks, hgrn_lb, hgrn_onorm, w_out, g_mix_pre, g_mix_post, g_mem, g_x_pre,
           g_x_post, wq_x, wk_x, wv_x, wo_x, g_ffn_pre, g_ffn_post, w_gate, w_up, w_down):
    B, T, D = x.shape
    assert D == D_MODEL and T % TOKEN_TILE == 0 and mem.shape == (B, MEM_LEN, D_MODEL)
    assert w_in.shape == (1, D_MODEL, D_IN) and hgrn_lb.shape[0] == 2
    tt = TOKEN_TILE
    nt = T // tt
    n_tiles = B * nt

    def row(g):
        return g.reshape(1, -1).astype(F32)

    tok_spec = pl.BlockSpec((1, tt, D_MODEL), lambda b, t: (b, t, 0))
    tok_shape = jax.ShapeDtypeStruct((B, T, D_MODEL), F32)

    def slab_spec(rows, cols, n_slabs):
        return pl.BlockSpec((rows, cols), lambda s: (jnp.minimum(s, n_slabs - 1), 0))

    def bf16_like(w):
        return jax.ShapeDtypeStruct(w.shape, BF16)

    w_in_f, w_out_f = w_in[0].astype(F32), w_out[0].astype(F32)
    in_rows = D_MODEL // B
    kt, v, w_in_bf, w_out_bf = pl.pallas_call(
        _mem_kv_kernel,
        grid=(B,),
        in_specs=[pl.BlockSpec((1, MEM_LEN, D_MODEL), lambda b: (b, 0, 0)),
                  _resident((1, D_MODEL)), _resident((D_MODEL, D_MODEL)), _resident((D_MODEL, D_MODEL)),
                  slab_spec(in_rows, D_IN, B), slab_spec(in_rows, D_MODEL, B)],
        out_specs=[pl.BlockSpec((1, D_MODEL, MEM_LEN), lambda b: (b, 0, 0)),
                   pl.BlockSpec((1, MEM_LEN, D_MODEL), lambda b: (b, 0, 0)),
                   slab_spec(in_rows, D_IN, B), slab_spec(in_rows, D_MODEL, B)],
        out_shape=[jax.ShapeDtypeStruct((B, D_MODEL, MEM_LEN), BF16),
                   jax.ShapeDtypeStruct((B, MEM_LEN, D_MODEL), BF16),
                   bf16_like(w_in_f), bf16_like(w_out_f)],
        scratch_shapes=[pltpu.VMEM((D_MODEL, D_MODEL), BF16), pltpu.VMEM((D_MODEL, D_MODEL), BF16)],
        compiler_params=_params(1),
        name="mem_kv",
    )(mem, row(g_mem[0]), wk_x[0].astype(F32), wv_x[0].astype(F32), w_in_f, w_out_f)

    def project_idx(s):
        i = jnp.minimum(s, n_tiles - 1)
        return (i // nt, i % nt, 0)

    def mix_idx(s):
        i = jnp.maximum(s - 1, 0)
        return (i // nt, i % nt, 0)

    next_w = [w[0].astype(F32) for w in (wq_x, wo_x, w_gate, w_up, w_down)]
    step_rows = D_MODEL // n_tiles
    down_rows = LANES
    n_down_slabs = D_FF // down_rows
    assert step_rows * n_tiles == D_MODEL and step_rows % 16 == 0 and n_down_slabs <= n_tiles
    next_w_specs = [slab_spec(step_rows, D_MODEL, n_tiles), slab_spec(step_rows, D_MODEL, n_tiles),
                    slab_spec(step_rows, D_FF, n_tiles), slab_spec(step_rows, D_FF, n_tiles),
                    slab_spec(down_rows, D_MODEL, n_down_slabs)]

    h1, wq_bf, wo_bf, wg_bf, wu_bf, wd_bf = pl.pallas_call(
        functools.partial(_mixer_kernel, nt, n_down_slabs),
        grid=(n_tiles + 1,),
        in_specs=[pl.BlockSpec(memory_space=pltpu.SMEM),
                  pl.BlockSpec((1, tt, D_MODEL), project_idx),
                  pl.BlockSpec((1, tt, D_MODEL), mix_idx),
                  _resident((2, HGRN_WIDTH)), _resident((1, HGRN_HEAD_DIM)),
                  _resident((1, D_MODEL)), _resident((1, D_MODEL)),
                  _resident((D_MODEL, D_IN)), _resident((D_MODEL, D_MODEL))] + next_w_specs,
        out_specs=[pl.BlockSpec((1, tt, D_MODEL), mix_idx)] + next_w_specs,
        out_shape=[tok_shape] + [bf16_like(w) for w in next_w],
        scratch_shapes=_proj_set_shapes(tt) + _proj_set_shapes(tt) + [
            pltpu.VMEM((2, WINDOW, LANES), BF16),
            pltpu.VMEM((2, WINDOW, LANES), BF16),
            pltpu.VMEM((HGRN_HEADS, HGRN_HEAD_DIM, HGRN_HEAD_DIM), F32),
            pltpu.VMEM((tt, D_MODEL), BF16),
        ],
        compiler_params=_params(1),
        name="mixer",
    )(sinks[0].astype(F32), x, x, hgrn_lb.astype(F32), row(hgrn_onorm[0]), row(g_mix_pre[0]),
      row(g_mix_post[0]), w_in_bf, w_out_bf, *next_w)

    h3 = pl.pallas_call(
        _xattn_ffn_kernel,
        grid=(B, nt),
        in_specs=[tok_spec,
                  pl.BlockSpec((1, D_MODEL, MEM_LEN), lambda b, t: (b, 0, 0)),
                  pl.BlockSpec((1, MEM_LEN, D_MODEL), lambda b, t: (b, 0, 0)),
                  _resident((1, D_MODEL)), _resident((1, D_MODEL)),
                  _resident((D_MODEL, D_MODEL)), _resident((D_MODEL, D_MODEL)),
                  _resident((1, D_MODEL)), _resident((1, D_MODEL)),
                  _resident((D_MODEL, D_FF)), _resident((D_MODEL, D_FF)), _resident((D_FF, D_MODEL))],
        out_specs=tok_spec,
        out_shape=tok_shape,
        scratch_shapes=[pltpu.VMEM((tt, D_MODEL), BF16)],
        compiler_params=_params(2),
        name="xattn_ffn",
    )(h1, kt, v, row(g_x_pre[0]), row(g_x_post[0]), wq_bf, wo_bf,
      row(g_ffn_pre[0]), row(g_ffn_post[0]), wg_bf, wu_bf, wd_bf)
    return h3
```

```python
import collections
import functools

import jax
import jax.numpy as jnp
from jax import lax
from jax.experimental import pallas as pl
from jax.experimental.pallas import tpu as pltpu

D_MODEL = 1024
CHUNK = 64
SWA_HEAD_DIM = 64
SWA_HEADS = 8
SWA_KV_HEADS = 2
SWA_WIDTH = SWA_HEADS * SWA_HEAD_DIM
SWA_KV_WIDTH = SWA_KV_HEADS * SWA_HEAD_DIM
WINDOW = 128
HGRN_HEAD_DIM = 128
HGRN_WIDTH = 512
HGRN_HEADS = HGRN_WIDTH // HGRN_HEAD_DIM
D_IN = SWA_WIDTH + 2 * SWA_KV_WIDTH + 4 * HGRN_WIDTH
HGRN_COL0 = SWA_WIDTH + 2 * SWA_KV_WIDTH
MEM_LEN = 256
XATTN_HEADS = 4
XATTN_HEAD_DIM = D_MODEL // XATTN_HEADS
D_FF = 2816
RMS_EPS = 1e-6
NEG_INF = -1e30
LOG2E = 1.4426950408889634

LANES = 128
Q_BLOCK = 128
TOKEN_TILE = 512
VMEM_LIMIT_BYTES = 56 * 1024 * 1024

F32 = jnp.float32
BF16 = jnp.bfloat16


def _rms_scale(x):
    return lax.rsqrt(jnp.mean(x * x, axis=-1, keepdims=True) + RMS_EPS)


def _sigmoid(x):
    return 1.0 / (1.0 + jnp.exp2(x * (-LOG2E)))


def _dot(a, b):
    return jnp.dot(a, b, preferred_element_type=F32)


def _dot_nt(a, b):
    return lax.dot_general(a, b, (((1,), (1,)), ((), ())), preferred_element_type=F32)


def _dot_tn(a, b):
    return lax.dot_general(a, b, (((0,), (0,)), ((), ())), preferred_element_type=F32)


def _cast_slabs(src_refs, dst_refs):
    for src, dst in zip(src_refs, dst_refs):
        dst[...] = src[...].astype(BF16)


def _mem_kv_kernel(mem_ref, g_ref, wk_ref, wv_ref, w_in_ref, w_out_ref,
                   kt_ref, v_ref, w_in_bf_ref, w_out_bf_ref, wk_bf, wv_bf):
    @pl.when(pl.program_id(0) == 0)
    def _():
        _cast_slabs((wk_ref, wv_ref), (wk_bf, wv_bf))

    _cast_slabs((w_in_ref, w_out_ref), (w_in_bf_ref, w_out_bf_ref))
    m = mem_ref[0]
    mn = (m * _rms_scale(m) * g_ref[...]).astype(BF16)
    k = _dot(mn, wk_bf[...])
    kt_ref[0] = jnp.transpose(k).astype(BF16)
    v_ref[0] = _dot(mn, wv_bf[...]).astype(BF16)


ProjSet = collections.namedtuple("ProjSet", "x zq k v qe ke qb kd dec iv sg")


def _proj_set_shapes(tt):
    return [
        pltpu.VMEM((tt, D_MODEL), F32),
        pltpu.VMEM((tt, SWA_WIDTH), BF16),
        pltpu.VMEM((2, WINDOW + tt, LANES), BF16),
        pltpu.VMEM((2, WINDOW + tt, LANES), BF16),
        pltpu.VMEM((tt, HGRN_WIDTH), BF16),
        pltpu.VMEM((tt, HGRN_WIDTH), BF16),
        pltpu.VMEM((tt, HGRN_WIDTH), BF16),
        pltpu.VMEM((tt, HGRN_WIDTH), BF16),
        pltpu.VMEM((tt // CHUNK, HGRN_WIDTH, HGRN_HEAD_DIM), F32),
        pltpu.VMEM((tt, HGRN_WIDTH), BF16),
        pltpu.VMEM((tt, HGRN_WIDTH), F32),
    ]


class _ProjectStage:
    def __init__(self, x_ref, gpre_ref, lbraw_ref, w_in_ref, ps):
        self.x_ref, self.gpre_ref, self.lbraw_ref, self.w_in_ref, self.ps = (
            x_ref, gpre_ref, lbraw_ref, w_in_ref, ps)
        self.tt = x_ref.shape[1]

    def _hgrn_cols(self, i):
        return _dot(self.u, self.w_in_ref[:, HGRN_COL0 + i * HGRN_WIDTH:HGRN_COL0 + (i + 1) * HGRN_WIDTH])

    def norm(self):
        x = self.x_ref[0]
        self.ps.x[...] = x
        self.u = (x * _rms_scale(x) * self.gpre_ref[...]).astype(BF16)

    def swa_q(self):
        zq = _dot(self.u, self.w_in_ref[:, 0:SWA_WIDTH])
        self.ps.zq[...] = (zq * (SWA_HEAD_DIM ** -0.5 * LOG2E)).astype(BF16)

    def swa_kv(self):
        ps = self.ps
        zkv = _dot(self.u, self.w_in_ref[:, SWA_WIDTH:HGRN_COL0])
        k = zkv[:, 0:SWA_KV_WIDTH]
        v = zkv[:, SWA_KV_WIDTH:2 * SWA_KV_WIDTH]
        ps.k[0, WINDOW:, :] = k.astype(BF16)
        ps.k[1, WINDOW:, :] = pltpu.roll(k, SWA_HEAD_DIM, axis=1).astype(BF16)
        ps.v[0, WINDOW:, :] = v.astype(BF16)
        ps.v[1, WINDOW:, :] = pltpu.roll(v, SWA_HEAD_DIM, axis=1).astype(BF16)

    def hgrn_q(self):
        qh = self._hgrn_cols(0)
        self.qf = qh * _sigmoid(qh) * (HGRN_HEAD_DIM ** -0.5)

    def hgrn_f(self):
        a = self.lbraw_ref[...]
        e = jnp.exp(a - jnp.max(a, axis=0, keepdims=True))
        lb = e[0:1] / jnp.sum(e, axis=0, keepdims=True)
        f = lb + (1.0 - lb) * _sigmoid(self._hgrn_cols(1))
        self.kf = 1.0 - f
        self.log2f = jnp.log2(f)

    def hgrn_i(self, half):
        w = HGRN_WIDTH // 2
        c0 = HGRN_COL0 + 2 * HGRN_WIDTH + half * w
        self.ps.iv[:, half * w:(half + 1) * w] = _dot(self.u, self.w_in_ref[:, c0:c0 + w]).astype(BF16)

    def hgrn_g(self):
        gh = self._hgrn_cols(3)
        self.ps.sg[...] = gh * _sigmoid(gh)

    def hgrn_decay(self):
        ps = self.ps
        cs = 4 * CHUNK
        ri = lax.broadcasted_iota(jnp.int32, (cs, cs), 0)
        ci = lax.broadcasted_iota(jnp.int32, (cs, cs), 1)
        tri = jnp.where((ri // CHUNK == ci // CHUNK) & (ci <= ri), 1.0, 0.0).astype(BF16)
        for rb in range(self.tt // cs):
            lf = self.log2f[rb * cs:(rb + 1) * cs]
            hi = lf.astype(BF16)
            r1 = lf - hi.astype(F32)
            mid = r1.astype(BF16)
            lo = (r1 - mid.astype(F32)).astype(BF16)
            b_blk = _dot(tri, hi) + _dot(tri, mid) + _dot(tri, lo)
            for cc in range(cs // CHUNK):
                c = rb * (cs // CHUNK) + cc
                rows = slice(c * CHUNK, (c + 1) * CHUNK)
                b = b_blk[cc * CHUNK:(cc + 1) * CHUNK]
                b_mid = b[CHUNK // 2 - 1:CHUNK // 2]
                b_last = b[CHUNK - 1:CHUNK]
                qe = self.qf[rows] * jnp.exp2(b - b_mid)
                ke = self.kf[rows] * jnp.exp2(b_mid - b)
                ps.qe[rows, :] = qe.astype(BF16)
                ps.ke[rows, :] = ke.astype(BF16)
                ps.qb[rows, :] = (qe * jnp.exp2(b_mid)).astype(BF16)
                ps.kd[rows, :] = (ke * jnp.exp2(b_last - b_mid)).astype(BF16)
                dec_rows = jnp.broadcast_to(jnp.exp2(b_last), (HGRN_HEAD_DIM, HGRN_WIDTH))
                ps.dec[c] = jnp.transpose(dec_rows)


class _MixStage:
    PAIR_BLOCKS = SWA_WIDTH // LANES
    PAIRS_PER_KV = PAIR_BLOCKS // SWA_KV_HEADS
    N_KEYS = WINDOW + Q_BLOCK

    def __init__(self, first, sinks_ref, onorm_ref, gpost_ref, w_out_ref, ps, kwin, vwin,
                 st_ref, y_ref, out_ref):
        self.first, self.sinks_ref, self.onorm_ref, self.gpost_ref = (
            first, sinks_ref, onorm_ref, gpost_ref)
        self.w_out_ref, self.ps, self.kwin, self.vwin, self.st_ref, self.y_ref, self.out_ref = (
            w_out_ref, ps, kwin, vwin, st_ref, y_ref, out_ref)
        self.tt = out_ref.shape[1]
        self.scores = {}
        self.amat = {}
        self.kv = {}

    def _parity(self, ver, bi):
        return ver ^ (bi // self.PAIRS_PER_KV)

    def swa_window(self):
        ps = self.ps
        zero_win = jnp.zeros((2, WINDOW, LANES), BF16)
        ps.k[:, 0:WINDOW, :] = jnp.where(self.first, zero_win, self.kwin[...])
        ps.v[:, 0:WINDOW, :] = jnp.where(self.first, zero_win, self.vwin[...])
        lane = lax.broadcasted_iota(jnp.int32, (Q_BLOCK, LANES), 1)
        self.low_half = lane < SWA_HEAD_DIM
        row_chunk = lax.broadcasted_iota(jnp.int32, (Q_BLOCK, self.N_KEYS), 0) // CHUNK + WINDOW // CHUNK
        col = lax.broadcasted_iota(jnp.int32, (Q_BLOCK, self.N_KEYS), 1)
        col_chunk = col // CHUNK
        self.band = (col_chunk <= row_chunk) & (col_chunk >= row_chunk - WINDOW // CHUNK)
        self.band_first = self.band & ((col >= WINDOW) | jnp.logical_not(self.first))
        col_q = lax.broadcasted_iota(jnp.int32, (CHUNK, LANES), 1)
        self.keep_low = col_q < CHUNK
        self.keep_high = col_q >= CHUNK

    def _mask_scores(self, j, s):
        if j == 0:
            return jnp.where(self.band_first, s, NEG_INF)
        top = jnp.concatenate([s[0:CHUNK, 0:LANES],
                               jnp.where(self.keep_low, s[0:CHUNK, LANES:2 * LANES], NEG_INF)], axis=1)
        bot = jnp.concatenate([jnp.where(self.keep_high, s[CHUNK:2 * CHUNK, 0:LANES], NEG_INF),
                               s[CHUNK:2 * CHUNK, LANES:2 * LANES]], axis=1)
        return jnp.concatenate([top, bot], axis=0)

    def swa_scores(self, j):
        r0 = j * Q_BLOCK
        qb = self.ps.zq[r0:r0 + Q_BLOCK, :]
        for ver in range(2):
            parts = []
            for bi in range(self.PAIR_BLOCKS):
                keep = self.low_half if self._parity(ver, bi) == 0 else jnp.logical_not(self.low_half)
                parts.append(jnp.where(keep, qb[:, bi * LANES:(bi + 1) * LANES], jnp.zeros((), BF16)))
            qs = jnp.concatenate(parts, axis=0)
            self.scores[j, ver] = _dot_nt(qs, self.ps.k[ver, r0:r0 + self.N_KEYS, :])

    def swa_finish(self, j):
        r0 = j * Q_BLOCK
        halves = [[None, None] for _ in range(self.PAIR_BLOCKS)]
        for ver in range(2):
            s = self.scores.pop((j, ver))
            probs, inv_ls = [], []
            for bi in range(self.PAIR_BLOCKS):
                sink = self.sinks_ref[2 * bi + self._parity(ver, bi)] * LOG2E
                sb = self._mask_scores(j, s[bi * Q_BLOCK:(bi + 1) * Q_BLOCK])
                m = jnp.maximum(jnp.max(sb, axis=-1, keepdims=True), sink)
                p = jnp.exp2(sb - m)
                l = jnp.sum(p, axis=-1, keepdims=True) + jnp.exp2(sink - m)
                probs.append(p.astype(BF16))
                inv_ls.append(1.0 / l)
            o = _dot(jnp.concatenate(probs, axis=0), self.ps.v[ver, r0:r0 + self.N_KEYS, :])
            for bi in range(self.PAIR_BLOCKS):
                halves[bi][self._parity(ver, bi)] = o[bi * Q_BLOCK:(bi + 1) * Q_BLOCK] * inv_ls[bi]
        for bi in range(self.PAIR_BLOCKS):
            blk = jnp.where(self.low_half, halves[bi][0], halves[bi][1])
            self.y_ref[r0:r0 + Q_BLOCK, bi * LANES:(bi + 1) * LANES] = blk.astype(BF16)

    def swa_save_window(self):
        self.kwin[...] = self.ps.k[:, self.tt:self.tt + WINDOW, :]
        self.vwin[...] = self.ps.v[:, self.tt:self.tt + WINDOW, :]

    @staticmethod
    def _chunk(hd, c):
        return slice(c * CHUNK, (c + 1) * CHUNK), slice(hd * HGRN_HEAD_DIM, (hd + 1) * HGRN_HEAD_DIM)

    def hgrn_local(self, pair):
        ps = self.ps
        ri = lax.broadcasted_iota(jnp.int32, (CHUNK, CHUNK), 0)
        ci = lax.broadcasted_iota(jnp.int32, (CHUNK, CHUNK), 1)
        causal = ci <= ri
        zero = jnp.zeros((CHUNK, HGRN_HEAD_DIM), BF16)
        hd_a, hd_b = 2 * pair, 2 * pair + 1
        for c in range(self.tt // CHUNK):
            rows, cols_a = self._chunk(hd_a, c)
            _, cols_b = self._chunk(hd_b, c)
            for hd, cols in ((hd_a, cols_a), (hd_b, cols_b)):
                a = _dot_nt(ps.qe[rows, cols], ps.ke[rows, cols])
                self.amat[hd, c] = jnp.where(causal, a, 0.0).astype(BF16)
            kd_ab = jnp.concatenate([ps.kd[rows, cols_a], ps.kd[rows, cols_b]], axis=0)
            iv_ab = jnp.concatenate([jnp.concatenate([ps.iv[rows, cols_a], zero], axis=1),
                                     jnp.concatenate([zero, ps.iv[rows, cols_b]], axis=1)], axis=0)
            kv = _dot_tn(kd_ab, iv_ab)
            self.kv[hd_a, c] = kv[:, 0:HGRN_HEAD_DIM]
            self.kv[hd_b, c] = kv[:, HGRN_HEAD_DIM:2 * HGRN_HEAD_DIM]

    def hgrn_state(self, hd):
        ps = self.ps
        onorm = self.onorm_ref[...]
        st = self.st_ref[hd] * jnp.where(self.first, 0.0, 1.0)
        for c in range(self.tt // CHUNK):
            rows, cols = self._chunk(hd, c)
            lhs = jnp.concatenate([ps.qb[rows, cols], self.amat.pop((hd, c))], axis=1)
            rhs = jnp.concatenate([st.astype(BF16), ps.iv[rows, cols]], axis=0)
            o = _dot(lhs, rhs)
            st = ps.dec[c, cols, :] * st + self.kv.pop((hd, c))
            on = o * _rms_scale(o) * onorm
            self.y_ref[rows, SWA_WIDTH + hd * HGRN_HEAD_DIM:SWA_WIDTH + (hd + 1) * HGRN_HEAD_DIM] = (
                on * self.ps.sg[rows, cols]).astype(BF16)
        self.st_ref[hd] = st

    def out_swa(self):
        self.y = _dot(self.y_ref[:, 0:SWA_WIDTH], self.w_out_ref[0:SWA_WIDTH, :])

    def out_hgrn(self):
        y = self.y + _dot(self.y_ref[:, SWA_WIDTH:D_MODEL], self.w_out_ref[SWA_WIDTH:D_MODEL, :])
        self.out_ref[0] = self.ps.x[...] + y * _rms_scale(y) * self.gpost_ref[...]


def _mixer_kernel(tiles_per_seq, n_down_slabs, sinks_ref, xa_ref, lbraw_ref, onorm_ref, gpre_ref,
                  gpost_ref, w_in_ref, w_out_ref, wq_ref, wo_ref, wg_ref, wu_ref, wd_ref,
                  out_ref, wq_bf_ref, wo_bf_ref, wg_bf_ref, wu_bf_ref, wd_bf_ref, *scratch):
    n_set = len(ProjSet._fields)
    sets = (ProjSet(*scratch[0:n_set]), ProjSet(*scratch[n_set:2 * n_set]))
    kwin, vwin, st_ref, y_ref = scratch[2 * n_set:]
    s = pl.program_id(0)

    _cast_slabs((wq_ref, wo_ref, wg_ref, wu_ref), (wq_bf_ref, wo_bf_ref, wg_bf_ref, wu_bf_ref))

    @pl.when(s < n_down_slabs)
    def _():
        _cast_slabs((wd_ref,), (wd_bf_ref,))

    @pl.when(s == 0)
    def _():
        for ref in sets[1] + (kwin, vwin, st_ref):
            ref[...] = jnp.zeros_like(ref)

    first = lax.rem(jnp.maximum(s - 1, 0), tiles_per_seq) == 0

    def body(write_set, read_set):
        pj = _ProjectStage(xa_ref, gpre_ref, lbraw_ref, w_in_ref, write_set)
        mx = _MixStage(first, sinks_ref, onorm_ref, gpost_ref, w_out_ref, read_set, kwin, vwin,
                       st_ref, y_ref, out_ref)
        mx.swa_window()
        pj.norm()
        mx.swa_scores(0)
        pj.hgrn_q()
        mx.swa_finish(0)
        mx.swa_scores(1)
        pj.hgrn_f()
        mx.swa_finish(1)
        mx.swa_scores(2)
        pj.hgrn_g()
        mx.swa_finish(2)
        mx.swa_scores(3)
        pj.hgrn_decay()
        mx.swa_finish(3)
        mx.swa_save_window()
        mx.hgrn_local(0)
        pj.swa_q()
        mx.hgrn_state(0)
        mx.hgrn_state(1)
        mx.hgrn_local(1)
        pj.swa_kv()
        mx.out_swa()
        mx.hgrn_state(2)
        pj.hgrn_i(0)
        mx.hgrn_state(3)
        mx.out_hgrn()
        pj.hgrn_i(1)

    @pl.when(lax.rem(s, 2) == 0)
    def _():
        body(sets[0], sets[1])

    @pl.when(lax.rem(s, 2) == 1)
    def _():
        body(sets[1], sets[0])


def _xattn_ffn_kernel(h_ref, kt_ref, v_ref, gxpre_ref, gxpost_ref, wq_ref, wo_ref,
                      gfpre_ref, gfpost_ref, wg_ref, wu_ref, wd_ref, out_ref, o_ref):
    half = h_ref.shape[1] // 2
    halves = (slice(0, half), slice(half, 2 * half))
    head_cols = [slice(hd * XATTN_HEAD_DIM, (hd + 1) * XATTN_HEAD_DIM) for hd in range(XATTN_HEADS)]

    def queries(rows):
        h = h_ref[0, rows, :]
        u = (h * _rms_scale(h) * gxpre_ref[...]).astype(BF16)
        return (_dot(u, wq_ref[...]) * (XATTN_HEAD_DIM ** -0.5 * LOG2E)).astype(BF16)

    def scores(q):
        return [_dot(q[:, cols], kt_ref[0, cols, :]) for cols in head_cols]

    def attend(rows, s_heads):
        for cols, s in zip(head_cols, s_heads):
            p = jnp.exp2(s - jnp.max(s, axis=-1, keepdims=True))
            l = jnp.sum(p, axis=-1, keepdims=True)
            o = _dot(p.astype(BF16), v_ref[0, :, cols]) * (1.0 / l)
            o_ref[rows, cols] = o.astype(BF16)

    def attn_out(rows):
        return _dot(o_ref[rows, :], wo_ref[...])

    def attn_residual(rows, y):
        return h_ref[0, rows, :] + y * _rms_scale(y) * gxpost_ref[...]

    def ffn_in(h2):
        return (h2 * _rms_scale(h2) * gfpre_ref[...]).astype(BF16)

    def ffn_residual(rows, h2, y):
        out_ref[0, rows, :] = h2 + y * _rms_scale(y) * gfpost_ref[...]

    q_a = queries(halves[0])
    s_a = scores(q_a)
    q_b = queries(halves[1])
    attend(halves[0], s_a)
    s_b = scores(q_b)
    y_a = attn_out(halves[0])
    attend(halves[1], s_b)
    h2_a = attn_residual(halves[0], y_a)
    y_b = attn_out(halves[1])
    u_a = ffn_in(h2_a)
    gate_a = _dot(u_a, wg_ref[...])
    up_a = _dot(u_a, wu_ref[...])
    h2_b = attn_residual(halves[1], y_b)
    u_b = ffn_in(h2_b)
    act_a = (gate_a * _sigmoid(gate_a) * up_a).astype(BF16)
    gate_b = _dot(u_b, wg_ref[...])
    z_a = _dot(act_a, wd_ref[...])
    up_b = _dot(u_b, wu_ref[...])
    ffn_residual(halves[0], h2_a, z_a)
    act_b = (gate_b * _sigmoid(gate_b) * up_b).astype(BF16)
    z_b = _dot(act_b, wd_ref[...])
    ffn_residual(halves[1], h2_b, z_b)


def _resident(shape):
    return pl.BlockSpec(shape, lambda *_: (0,) * len(shape), pipeline_mode=pl.Buffered(1))


def _params(n_axes):
    return pltpu.CompilerParams(dimension_semantics=("arbitrary",) * n_axes,
                                vmem_limit_bytes=VMEM_LIMIT_BYTES)


def kernel(x, mem, w_in, sinks, hgrn_lb, hgrn_onorm, w_out, g_mix_pre, g_mix_post, g_mem, g_x_pre,
           g_x_post, wq_x, wk_x, wv_x, wo_x, g_ffn_pre, g_ffn_post, w_gate, w_up, w_down):
    B, T, D = x.shape
    assert D == D_MODEL and T % TOKEN_TILE == 0 and mem.shape == (B, MEM_LEN, D_MODEL)
    assert w_in.shape == (1, D_MODEL, D_IN) and hgrn_lb.shape[0] == 2
    tt = TOKEN_TILE
    nt = T // tt
    n_tiles = B * nt

    def row(g):
        return g.reshape(1, -1).astype(F32)

    tok_spec = pl.BlockSpec((1, tt, D_MODEL), lambda b, t: (b, t, 0))
    tok_shape = jax.ShapeDtypeStruct((B, T, D_MODEL), F32)

    def slab_spec(rows, cols, n_slabs):
        return pl.BlockSpec((rows, cols), lambda s: (jnp.minimum(s, n_slabs - 1), 0))

    def bf16_like(w):
        return jax.ShapeDtypeStruct(w.shape, BF16)

    w_in_f, w_out_f = w_in[0].astype(F32), w_out[0].astype(F32)
    in_rows = D_MODEL // B
    kt, v, w_in_bf, w_out_bf = pl.pallas_call(
        _mem_kv_kernel,
        grid=(B,),
        in_specs=[pl.BlockSpec((1, MEM_LEN, D_MODEL), lambda b: (b, 0, 0)),
                  _resident((1, D_MODEL)), _resident((D_MODEL, D_MODEL)), _resident((D_MODEL, D_MODEL)),
                  slab_spec(in_rows, D_IN, B), slab_spec(in_rows, D_MODEL, B)],
        out_specs=[pl.BlockSpec((1, D_MODEL, MEM_LEN), lambda b: (b, 0, 0)),
                   pl.BlockSpec((1, MEM_LEN, D_MODEL), lambda b: (b, 0, 0)),
                   slab_spec(in_rows, D_IN, B), slab_spec(in_rows, D_MODEL, B)],
        out_shape=[jax.ShapeDtypeStruct((B, D_MODEL, MEM_LEN), BF16),
                   jax.ShapeDtypeStruct((B, MEM_LEN, D_MODEL), BF16),
                   bf16_like(w_in_f), bf16_like(w_out_f)],
        scratch_shapes=[pltpu.VMEM((D_MODEL, D_MODEL), BF16), pltpu.VMEM((D_MODEL, D_MODEL), BF16)],
        compiler_params=_params(1),
        name="mem_kv",
    )(mem, row(g_mem[0]), wk_x[0].astype(F32), wv_x[0].astype(F32), w_in_f, w_out_f)

    def project_idx(s):
        i = jnp.minimum(s, n_tiles - 1)
        return (i // nt, i % nt, 0)

    def mix_idx(s):
        i = jnp.maximum(s - 1, 0)
        return (i // nt, i % nt, 0)

    next_w = [w[0].astype(F32) for w in (wq_x, wo_x, w_gate, w_up, w_down)]
    step_rows = D_MODEL // n_tiles
    down_rows = LANES
    n_down_slabs = D_FF // down_rows
    assert step_rows * n_tiles == D_MODEL and step_rows % 16 == 0 and n_down_slabs <= n_tiles
    next_w_specs = [slab_spec(step_rows, D_MODEL, n_tiles), slab_spec(step_rows, D_MODEL, n_tiles),
                    slab_spec(step_rows, D_FF, n_tiles), slab_spec(step_rows, D_FF, n_tiles),
                    slab_spec(down_rows, D_MODEL, n_down_slabs)]

    h1, wq_bf, wo_bf, wg_bf, wu_bf, wd_bf = pl.pallas_call(
        functools.partial(_mixer_kernel, nt, n_down_slabs),
        grid=(n_tiles + 1,),
        in_specs=[pl.BlockSpec(memory_space=pltpu.SMEM),
                  pl.BlockSpec((1, tt, D_MODEL), project_idx),
                  _resident((2, HGRN_WIDTH)), _resident((1, HGRN_HEAD_DIM)),
                  _resident((1, D_MODEL)), _resident((1, D_MODEL)),
                  _resident((D_MODEL, D_IN)), _resident((D_MODEL, D_MODEL))] + next_w_specs,
        out_specs=[pl.BlockSpec((1, tt, D_MODEL), mix_idx)] + next_w_specs,
        out_shape=[tok_shape] + [bf16_like(w) for w in next_w],
        scratch_shapes=_proj_set_shapes(tt) + _proj_set_shapes(tt) + [
            pltpu.VMEM((2, WINDOW, LANES), BF16),
            pltpu.VMEM((2, WINDOW, LANES), BF16),
            pltpu.VMEM((HGRN_HEADS, HGRN_HEAD_DIM, HGRN_HEAD_DIM), F32),
            pltpu.VMEM((tt, D_MODEL), BF16),
        ],
        compiler_params=_params(1),
        name="mixer",
    )(sinks[0].astype(F32), x, hgrn_lb.astype(F32), row(hgrn_onorm[0]), row(g_mix_pre[0]),
      row(g_mix_post[0]), w_in_bf, w_out_bf, *next_w)

    h3 = pl.pallas_call(
        _xattn_ffn_kernel,
        grid=(B, nt),
        in_specs=[tok_spec,
                  pl.BlockSpec((1, D_MODEL, MEM_LEN), lambda b, t: (b, 0, 0)),
                  pl.BlockSpec((1, MEM_LEN, D_MODEL), lambda b, t: (b, 0, 0)),
                  _resident((1, D_MODEL)), _resident((1, D_MODEL)),
                  _resident((D_MODEL, D_MODEL)), _resident((D_MODEL, D_MODEL)),
                  _resident((1, D_MODEL)), _resident((1, D_MODEL)),
                  _resident((D_MODEL, D_FF)), _resident((D_MODEL, D_FF)), _resident((D_FF, D_MODEL))],
        out_specs=tok_spec,
        out_shape=tok_shape,
        scratch_shapes=[pltpu.VMEM((tt, D_MODEL), BF16)],
        compiler_params=_params(2),
        name="xattn_ffn",
    )(h1, kt, v, row(g_x_pre[0]), row(g_x_post[0]), wq_bf, wo_bf,
      row(g_ffn_pre[0]), row(g_ffn_post[0]), wg_bf, wu_bf, wd_bf)
    return h3
```

```python
import collections
import functools

import jax
import jax.numpy as jnp
from jax import lax
from jax.experimental import pallas as pl
from jax.experimental.pallas import tpu as pltpu

D_MODEL = 1024
CHUNK = 64
SWA_HEAD_DIM = 64
SWA_HEADS = 8
SWA_KV_HEADS = 2
SWA_WIDTH = SWA_HEADS * SWA_HEAD_DIM
SWA_KV_WIDTH = SWA_KV_HEADS * SWA_HEAD_DIM
WINDOW = 128
HGRN_HEAD_DIM = 128
HGRN_WIDTH = 512
HGRN_HEADS = HGRN_WIDTH // HGRN_HEAD_DIM
D_IN = SWA_WIDTH + 2 * SWA_KV_WIDTH + 4 * HGRN_WIDTH
HGRN_COL0 = SWA_WIDTH + 2 * SWA_KV_WIDTH
MEM_LEN = 256
XATTN_HEADS = 4
XATTN_HEAD_DIM = D_MODEL // XATTN_HEADS
D_FF = 2816
RMS_EPS = 1e-6
NEG_INF = -1e30
LOG2E = 1.4426950408889634

LANES = 128
Q_BLOCK = 128
TOKEN_TILE = 512
XFFN_TILE = 1024
XFFN_PIECE = 256
VMEM_LIMIT_BYTES = 56 * 1024 * 1024

F32 = jnp.float32
BF16 = jnp.bfloat16


def _rms_scale(x):
    return lax.rsqrt(jnp.mean(x * x, axis=-1, keepdims=True) + RMS_EPS)


def _sigmoid(x):
    return 1.0 / (1.0 + jnp.exp2(x * (-LOG2E)))


def _dot(a, b):
    return jnp.dot(a, b, preferred_element_type=F32)


def _dot_nt(a, b):
    return lax.dot_general(a, b, (((1,), (1,)), ((), ())), preferred_element_type=F32)


def _dot_tn(a, b):
    return lax.dot_general(a, b, (((0,), (0,)), ((), ())), preferred_element_type=F32)


def _cast_slabs(src_refs, dst_refs):
    for src, dst in zip(src_refs, dst_refs):
        dst[...] = src[...].astype(BF16)


def _mem_kv_kernel(mem_ref, g_ref, wk_ref, wv_ref, w_in_ref, w_out_ref,
                   kt_ref, v_ref, w_in_bf_ref, w_out_bf_ref, wk_bf, wv_bf):
    @pl.when(pl.program_id(0) == 0)
    def _():
        _cast_slabs((wk_ref, wv_ref), (wk_bf, wv_bf))

    _cast_slabs((w_in_ref, w_out_ref), (w_in_bf_ref, w_out_bf_ref))
    m = mem_ref[0]
    mn = (m * _rms_scale(m) * g_ref[...]).astype(BF16)
    k = _dot(mn, wk_bf[...])
    kt_ref[0] = jnp.transpose(k).astype(BF16)
    v_ref[0] = _dot(mn, wv_bf[...]).astype(BF16)


ProjSet = collections.namedtuple("ProjSet", "x zq k v qe ke qb kd dec iv sg")


def _proj_set_shapes(tt):
    return [
        pltpu.VMEM((tt, D_MODEL), F32),
        pltpu.VMEM((tt, SWA_WIDTH), BF16),
        pltpu.VMEM((2, WINDOW + tt, LANES), BF16),
        pltpu.VMEM((2, WINDOW + tt, LANES), BF16),
        pltpu.VMEM((tt, HGRN_WIDTH), BF16),
        pltpu.VMEM((tt, HGRN_WIDTH), BF16),
        pltpu.VMEM((tt, HGRN_WIDTH), BF16),
        pltpu.VMEM((tt, HGRN_WIDTH), BF16),
        pltpu.VMEM((tt // CHUNK, HGRN_WIDTH, HGRN_HEAD_DIM), F32),
        pltpu.VMEM((tt, HGRN_WIDTH), BF16),
        pltpu.VMEM((tt, HGRN_WIDTH), F32),
    ]


class _ProjectStage:
    def __init__(self, x_ref, gpre_ref, lbraw_ref, w_in_ref, ps):
        self.x_ref, self.gpre_ref, self.lbraw_ref, self.w_in_ref, self.ps = (
            x_ref, gpre_ref, lbraw_ref, w_in_ref, ps)
        self.tt = x_ref.shape[1]

    def _hgrn_cols(self, i):
        return _dot(self.u, self.w_in_ref[:, HGRN_COL0 + i * HGRN_WIDTH:HGRN_COL0 + (i + 1) * HGRN_WIDTH])

    def norm(self):
        x = self.x_ref[0]
        self.ps.x[...] = x
        self.u = (x * _rms_scale(x) * self.gpre_ref[...]).astype(BF16)

    def swa_q(self):
        zq = _dot(self.u, self.w_in_ref[:, 0:SWA_WIDTH])
        self.ps.zq[...] = (zq * (SWA_HEAD_DIM ** -0.5 * LOG2E)).astype(BF16)

    def swa_kv(self):
        ps = self.ps
        zkv = _dot(self.u, self.w_in_ref[:, SWA_WIDTH:HGRN_COL0])
        k = zkv[:, 0:SWA_KV_WIDTH]
        v = zkv[:, SWA_KV_WIDTH:2 * SWA_KV_WIDTH]
        ps.k[0, WINDOW:, :] = k.astype(BF16)
        ps.k[1, WINDOW:, :] = pltpu.roll(k, SWA_HEAD_DIM, axis=1).astype(BF16)
        ps.v[0, WINDOW:, :] = v.astype(BF16)
        ps.v[1, WINDOW:, :] = pltpu.roll(v, SWA_HEAD_DIM, axis=1).astype(BF16)

    def hgrn_q(self):
        qh = self._hgrn_cols(0)
        self.qf = qh * _sigmoid(qh) * (HGRN_HEAD_DIM ** -0.5)

    def hgrn_f(self):
        a = self.lbraw_ref[...]
        e = jnp.exp(a - jnp.max(a, axis=0, keepdims=True))
        lb = e[0:1] / jnp.sum(e, axis=0, keepdims=True)
        f = lb + (1.0 - lb) * _sigmoid(self._hgrn_cols(1))
        self.kf = 1.0 - f
        self.log2f = jnp.log2(f)

    def hgrn_i(self, half):
        w = HGRN_WIDTH // 2
        c0 = HGRN_COL0 + 2 * HGRN_WIDTH + half * w
        self.ps.iv[:, half * w:(half + 1) * w] = _dot(self.u, self.w_in_ref[:, c0:c0 + w]).astype(BF16)

    def hgrn_g(self):
        gh = self._hgrn_cols(3)
        self.ps.sg[...] = gh * _sigmoid(gh)

    def hgrn_decay(self):
        ps = self.ps
        cs = 4 * CHUNK
        ri = lax.broadcasted_iota(jnp.int32, (cs, cs), 0)
        ci = lax.broadcasted_iota(jnp.int32, (cs, cs), 1)
        tri = jnp.where((ri // CHUNK == ci // CHUNK) & (ci <= ri), 1.0, 0.0).astype(BF16)
        for rb in range(self.tt // cs):
            lf = self.log2f[rb * cs:(rb + 1) * cs]
            hi = lf.astype(BF16)
            r1 = lf - hi.astype(F32)
            mid = r1.astype(BF16)
            lo = (r1 - mid.astype(F32)).astype(BF16)
            b_blk = _dot(tri, hi) + _dot(tri, mid) + _dot(tri, lo)
            for cc in range(cs // CHUNK):
                c = rb * (cs // CHUNK) + cc
                rows = slice(c * CHUNK, (c + 1) * CHUNK)
                b = b_blk[cc * CHUNK:(cc + 1) * CHUNK]
                b_mid = b[CHUNK // 2 - 1:CHUNK // 2]
                b_last = b[CHUNK - 1:CHUNK]
                qe = self.qf[rows] * jnp.exp2(b - b_mid)
                ke = self.kf[rows] * jnp.exp2(b_mid - b)
                ps.qe[rows, :] = qe.astype(BF16)
                ps.ke[rows, :] = ke.astype(BF16)
                ps.qb[rows, :] = (qe * jnp.exp2(b_mid)).astype(BF16)
                ps.kd[rows, :] = (ke * jnp.exp2(b_last - b_mid)).astype(BF16)
                dec_rows = jnp.broadcast_to(jnp.exp2(b_last), (HGRN_HEAD_DIM, HGRN_WIDTH))
                ps.dec[c] = jnp.transpose(dec_rows)


class _MixStage:
    PAIR_BLOCKS = SWA_WIDTH // LANES
    PAIRS_PER_KV = PAIR_BLOCKS // SWA_KV_HEADS
    N_KEYS = WINDOW + Q_BLOCK

    def __init__(self, first, sinks_ref, onorm_ref, gpost_ref, w_out_ref, ps, kwin, vwin,
                 st_ref, y_ref, out_ref):
        self.first, self.sinks_ref, self.onorm_ref, self.gpost_ref = (
            first, sinks_ref, onorm_ref, gpost_ref)
        self.w_out_ref, self.ps, self.kwin, self.vwin, self.st_ref, self.y_ref, self.out_ref = (
            w_out_ref, ps, kwin, vwin, st_ref, y_ref, out_ref)
        self.tt = out_ref.shape[1]
        self.scores = {}
        self.amat = {}
        self.kv = {}

    def _parity(self, ver, bi):
        return ver ^ (bi // self.PAIRS_PER_KV)

    def swa_window(self):
        ps = self.ps
        zero_win = jnp.zeros((2, WINDOW, LANES), BF16)
        ps.k[:, 0:WINDOW, :] = jnp.where(self.first, zero_win, self.kwin[...])
        ps.v[:, 0:WINDOW, :] = jnp.where(self.first, zero_win, self.vwin[...])
        lane = lax.broadcasted_iota(jnp.int32, (Q_BLOCK, LANES), 1)
        self.low_half = lane < SWA_HEAD_DIM
        row_chunk = lax.broadcasted_iota(jnp.int32, (Q_BLOCK, self.N_KEYS), 0) // CHUNK + WINDOW // CHUNK
        col = lax.broadcasted_iota(jnp.int32, (Q_BLOCK, self.N_KEYS), 1)
        col_chunk = col // CHUNK
        self.band = (col_chunk <= row_chunk) & (col_chunk >= row_chunk - WINDOW // CHUNK)
        self.band_first = self.band & ((col >= WINDOW) | jnp.logical_not(self.first))
        col_q = lax.broadcasted_iota(jnp.int32, (CHUNK, LANES), 1)
        self.keep_low = col_q < CHUNK
        self.keep_high = col_q >= CHUNK

    def _mask_scores(self, j, s):
        if j == 0:
            return jnp.where(self.band_first, s, NEG_INF)
        top = jnp.concatenate([s[0:CHUNK, 0:LANES],
                               jnp.where(self.keep_low, s[0:CHUNK, LANES:2 * LANES], NEG_INF)], axis=1)
        bot = jnp.concatenate([jnp.where(self.keep_high, s[CHUNK:2 * CHUNK, 0:LANES], NEG_INF),
                               s[CHUNK:2 * CHUNK, LANES:2 * LANES]], axis=1)
        return jnp.concatenate([top, bot], axis=0)

    def swa_scores(self, j):
        r0 = j * Q_BLOCK
        qb = self.ps.zq[r0:r0 + Q_BLOCK, :]
        for ver in range(2):
            parts = []
            for bi in range(self.PAIR_BLOCKS):
                keep = self.low_half if self._parity(ver, bi) == 0 else jnp.logical_not(self.low_half)
                parts.append(jnp.where(keep, qb[:, bi * LANES:(bi + 1) * LANES], jnp.zeros((), BF16)))
            qs = jnp.concatenate(parts, axis=0)
            self.scores[j, ver] = _dot_nt(qs, self.ps.k[ver, r0:r0 + self.N_KEYS, :])

    def swa_finish(self, j):
        r0 = j * Q_BLOCK
        halves = [[None, None] for _ in range(self.PAIR_BLOCKS)]
        for ver in range(2):
            s = self.scores.pop((j, ver))
            probs, inv_ls = [], []
            for bi in range(self.PAIR_BLOCKS):
                sink = self.sinks_ref[2 * bi + self._parity(ver, bi)] * LOG2E
                sb = self._mask_scores(j, s[bi * Q_BLOCK:(bi + 1) * Q_BLOCK])
                m = jnp.maximum(jnp.max(sb, axis=-1, keepdims=True), sink)
                p = jnp.exp2(sb - m)
                l = jnp.sum(p, axis=-1, keepdims=True) + jnp.exp2(sink - m)
                probs.append(p.astype(BF16))
                inv_ls.append(1.0 / l)
            o = _dot(jnp.concatenate(probs, axis=0), self.ps.v[ver, r0:r0 + self.N_KEYS, :])
            for bi in range(self.PAIR_BLOCKS):
                halves[bi][self._parity(ver, bi)] = o[bi * Q_BLOCK:(bi + 1) * Q_BLOCK] * inv_ls[bi]
        for bi in range(self.PAIR_BLOCKS):
            blk = jnp.where(self.low_half, halves[bi][0], halves[bi][1])
            self.y_ref[r0:r0 + Q_BLOCK, bi * LANES:(bi + 1) * LANES] = blk.astype(BF16)

    def swa_save_window(self):
        self.kwin[...] = self.ps.k[:, self.tt:self.tt + WINDOW, :]
        self.vwin[...] = self.ps.v[:, self.tt:self.tt + WINDOW, :]

    @staticmethod
    def _chunk(hd, c):
        return slice(c * CHUNK, (c + 1) * CHUNK), slice(hd * HGRN_HEAD_DIM, (hd + 1) * HGRN_HEAD_DIM)

    def hgrn_local(self, pair):
        ps = self.ps
        ri = lax.broadcasted_iota(jnp.int32, (CHUNK, CHUNK), 0)
        ci = lax.broadcasted_iota(jnp.int32, (CHUNK, CHUNK), 1)
        causal = ci <= ri
        zero = jnp.zeros((CHUNK, HGRN_HEAD_DIM), BF16)
        hd_a, hd_b = 2 * pair, 2 * pair + 1
        for c in range(self.tt // CHUNK):
            rows, cols_a = self._chunk(hd_a, c)
            _, cols_b = self._chunk(hd_b, c)
            for hd, cols in ((hd_a, cols_a), (hd_b, cols_b)):
                a = _dot_nt(ps.qe[rows, cols], ps.ke[rows, cols])
                self.amat[hd, c] = jnp.where(causal, a, 0.0).astype(BF16)
            kd_ab = jnp.concatenate([ps.kd[rows, cols_a], ps.kd[rows, cols_b]], axis=0)
            iv_ab = jnp.concatenate([jnp.concatenate([ps.iv[rows, cols_a], zero], axis=1),
                                     jnp.concatenate([zero, ps.iv[rows, cols_b]], axis=1)], axis=0)
            kv = _dot_tn(kd_ab, iv_ab)
            self.kv[hd_a, c] = kv[:, 0:HGRN_HEAD_DIM]
            self.kv[hd_b, c] = kv[:, HGRN_HEAD_DIM:2 * HGRN_HEAD_DIM]

    def hgrn_state(self, hd):
        ps = self.ps
        onorm = self.onorm_ref[...]
        st = self.st_ref[hd] * jnp.where(self.first, 0.0, 1.0)
        for c in range(self.tt // CHUNK):
            rows, cols = self._chunk(hd, c)
            lhs = jnp.concatenate([ps.qb[rows, cols], self.amat.pop((hd, c))], axis=1)
            rhs = jnp.concatenate([st.astype(BF16), ps.iv[rows, cols]], axis=0)
            o = _dot(lhs, rhs)
            st = ps.dec[c, cols, :] * st + self.kv.pop((hd, c))
            on = o * _rms_scale(o) * onorm
            self.y_ref[rows, SWA_WIDTH + hd * HGRN_HEAD_DIM:SWA_WIDTH + (hd + 1) * HGRN_HEAD_DIM] = (
                on * self.ps.sg[rows, cols]).astype(BF16)
        self.st_ref[hd] = st

    def out_swa(self):
        self.y = _dot(self.y_ref[:, 0:SWA_WIDTH], self.w_out_ref[0:SWA_WIDTH, :])

    def out_hgrn(self):
        y = self.y + _dot(self.y_ref[:, SWA_WIDTH:D_MODEL], self.w_out_ref[SWA_WIDTH:D_MODEL, :])
        self.out_ref[0] = self.ps.x[...] + y * _rms_scale(y) * self.gpost_ref[...]


def _mixer_kernel(tiles_per_seq, n_down_slabs, sinks_ref, xa_ref, lbraw_ref, onorm_ref, gpre_ref,
                  gpost_ref, w_in_ref, w_out_ref, wq_ref, wo_ref, wg_ref, wu_ref, wd_ref,
                  out_ref, wq_bf_ref, wo_bf_ref, wg_bf_ref, wu_bf_ref, wd_bf_ref, *scratch):
    n_set = len(ProjSet._fields)
    sets = (ProjSet(*scratch[0:n_set]), ProjSet(*scratch[n_set:2 * n_set]))
    kwin, vwin, st_ref, y_ref = scratch[2 * n_set:]
    s = pl.program_id(0)

    _cast_slabs((wq_ref, wo_ref, wg_ref, wu_ref), (wq_bf_ref, wo_bf_ref, wg_bf_ref, wu_bf_ref))

    @pl.when(s < n_down_slabs)
    def _():
        _cast_slabs((wd_ref,), (wd_bf_ref,))

    @pl.when(s == 0)
    def _():
        for ref in sets[1] + (kwin, vwin, st_ref):
            ref[...] = jnp.zeros_like(ref)

    first = lax.rem(jnp.maximum(s - 1, 0), tiles_per_seq) == 0

    def body(write_set, read_set):
        pj = _ProjectStage(xa_ref, gpre_ref, lbraw_ref, w_in_ref, write_set)
        mx = _MixStage(first, sinks_ref, onorm_ref, gpost_ref, w_out_ref, read_set, kwin, vwin,
                       st_ref, y_ref, out_ref)
        mx.swa_window()
        pj.norm()
        mx.swa_scores(0)
        pj.hgrn_q()
        mx.swa_finish(0)
        mx.swa_scores(1)
        pj.hgrn_f()
        mx.swa_finish(1)
        mx.swa_scores(2)
        pj.hgrn_g()
        mx.swa_finish(2)
        mx.swa_scores(3)
        pj.hgrn_decay()
        mx.swa_finish(3)
        mx.swa_save_window()
        mx.hgrn_local(0)
        pj.swa_q()
        mx.hgrn_state(0)
        mx.hgrn_state(1)
        mx.hgrn_local(1)
        pj.swa_kv()
        mx.out_swa()
        mx.hgrn_state(2)
        pj.hgrn_i(0)
        mx.hgrn_state(3)
        mx.out_hgrn()
        pj.hgrn_i(1)

    @pl.when(lax.rem(s, 2) == 0)
    def _():
        body(sets[0], sets[1])

    @pl.when(lax.rem(s, 2) == 1)
    def _():
        body(sets[1], sets[0])


def _xattn_ffn_kernel(h_ref, kt_ref, v_ref, gxpre_ref, gxpost_ref, wq_ref, wo_ref,
                      gfpre_ref, gfpost_ref, wg_ref, wu_ref, wd_ref, out_ref, o_ref):
    n_pieces = h_ref.shape[1] // XFFN_PIECE
    pieces = [slice(i * XFFN_PIECE, (i + 1) * XFFN_PIECE) for i in range(n_pieces)]
    head_cols = [slice(hd * XATTN_HEAD_DIM, (hd + 1) * XATTN_HEAD_DIM) for hd in range(XATTN_HEADS)]

    def queries(rows):
        h = h_ref[0, rows, :]
        u = (h * _rms_scale(h) * gxpre_ref[...]).astype(BF16)
        return (_dot(u, wq_ref[...]) * (XATTN_HEAD_DIM ** -0.5 * LOG2E)).astype(BF16)

    def scores(q):
        return [_dot(q[:, cols], kt_ref[0, cols, :]) for cols in head_cols]

    def attend(rows, s_heads):
        for cols, s in zip(head_cols, s_heads):
            p = jnp.exp2(s - jnp.max(s, axis=-1, keepdims=True))
            l = jnp.sum(p, axis=-1, keepdims=True)
            o = _dot(p.astype(BF16), v_ref[0, :, cols]) * (1.0 / l)
            o_ref[rows, cols] = o.astype(BF16)

    def attn_out(rows):
        return _dot(o_ref[rows, :], wo_ref[...])

    def attn_residual(rows, y):
        return h_ref[0, rows, :] + y * _rms_scale(y) * gxpost_ref[...]

    def ffn_in(h2):
        return (h2 * _rms_scale(h2) * gfpre_ref[...]).astype(BF16)

    def ffn_residual(rows, h2, y):
        out_ref[0, rows, :] = h2 + y * _rms_scale(y) * gfpost_ref[...]

    val = {}

    def attn_ops(a, b):
        def q(i):
            return lambda: val.__setitem__(("q", i), queries(pieces[i]))

        def s(i):
            return lambda: val.__setitem__(("s", i), scores(val.pop(("q", i))))

        def att(i):
            return lambda: attend(pieces[i], val.pop(("s", i)))

        def y(i):
            return lambda: val.__setitem__(("y", i), attn_out(pieces[i]))

        def h2u(i):
            def run():
                h2 = attn_residual(pieces[i], val.pop(("y", i)))
                val["h2", i] = h2
                val["u", i] = ffn_in(h2)
            return run

        return [q(a), s(a), q(b), att(a), s(b), y(a), att(b), h2u(a), y(b), h2u(b)]

    def ffn_ops(a, b):
        def gate(i):
            return lambda: val.__setitem__(("gate", i), _dot(val["u", i], wg_ref[...]))

        def up(i):
            return lambda: val.__setitem__(("up", i), _dot(val.pop(("u", i)), wu_ref[...]))

        def act(i):
            def run():
                g = val.pop(("gate", i))
                val["act", i] = (g * _sigmoid(g) * val.pop(("up", i))).astype(BF16)
            return run

        def down(i):
            return lambda: val.__setitem__(("z", i), _dot(val.pop(("act", i)), wd_ref[...]))

        def res(i):
            return lambda: ffn_residual(pieces[i], val.pop(("h2", i)), val.pop(("z", i)))

        return [gate(a), up(a), act(a), gate(b), down(a), up(b), res(a), act(b), down(b), res(b)]

    pairs = [(i, i + 1) for i in range(0, n_pieces, 2)]
    for op in attn_ops(*pairs[0]):
        op()
    for k, pair in enumerate(pairs):
        f_ops = ffn_ops(*pair)
        x_ops = attn_ops(*pairs[k + 1]) if k + 1 < len(pairs) else []
        for j, f_op in enumerate(f_ops):
            f_op()
            if j < len(x_ops):
                x_ops[j]()


def _resident(shape):
    return pl.BlockSpec(shape, lambda *_: (0,) * len(shape), pipeline_mode=pl.Buffered(1))


def _params(n_axes):
    return pltpu.CompilerParams(dimension_semantics=("arbitrary",) * n_axes,
                                vmem_limit_bytes=VMEM_LIMIT_BYTES)


def kernel(x, mem, w_in, sinks, hgrn_lb, hgrn_onorm, w_out, g_mix_pre, g_mix_post, g_mem, g_x_pre,
           g_x_post, wq_x, wk_x, wv_x, wo_x, g_ffn_pre, g_ffn_post, w_gate, w_up, w_down):
    B, T, D = x.shape
    assert D == D_MODEL and T % TOKEN_TILE == 0 and mem.shape == (B, MEM_LEN, D_MODEL)
    assert w_in.shape == (1, D_MODEL, D_IN) and hgrn_lb.shape[0] == 2
    tt = TOKEN_TILE
    nt = T // tt
    n_tiles = B * nt

    def row(g):
        return g.reshape(1, -1).astype(F32)

    tok_shape = jax.ShapeDtypeStruct((B, T, D_MODEL), F32)

    def slab_spec(rows, cols, n_slabs):
        return pl.BlockSpec((rows, cols), lambda s: (jnp.minimum(s, n_slabs - 1), 0))

    def bf16_like(w):
        return jax.ShapeDtypeStruct(w.shape, BF16)

    w_in_f, w_out_f = w_in[0].astype(F32), w_out[0].astype(F32)
    in_rows = D_MODEL // B
    kt, v, w_in_bf, w_out_bf = pl.pallas_call(
        _mem_kv_kernel,
        grid=(B,),
        in_specs=[pl.BlockSpec((1, MEM_LEN, D_MODEL), lambda b: (b, 0, 0)),
                  _resident((1, D_MODEL)), _resident((D_MODEL, D_MODEL)), _resident((D_MODEL, D_MODEL)),
                  slab_spec(in_rows, D_IN, B), slab_spec(in_rows, D_MODEL, B)],
        out_specs=[pl.BlockSpec((1, D_MODEL, MEM_LEN), lambda b: (b, 0, 0)),
                   pl.BlockSpec((1, MEM_LEN, D_MODEL), lambda b: (b, 0, 0)),
                   slab_spec(in_rows, D_IN, B), slab_spec(in_rows, D_MODEL, B)],
        out_shape=[jax.ShapeDtypeStruct((B, D_MODEL, MEM_LEN), BF16),
                   jax.ShapeDtypeStruct((B, MEM_LEN, D_MODEL), BF16),
                   bf16_like(w_in_f), bf16_like(w_out_f)],
        scratch_shapes=[pltpu.VMEM((D_MODEL, D_MODEL), BF16), pltpu.VMEM((D_MODEL, D_MODEL), BF16)],
        compiler_params=_params(1),
        name="mem_kv",
    )(mem, row(g_mem[0]), wk_x[0].astype(F32), wv_x[0].astype(F32), w_in_f, w_out_f)

    def project_idx(s):
        i = jnp.minimum(s, n_tiles - 1)
        return (i // nt, i % nt, 0)

    def mix_idx(s):
        i = jnp.maximum(s - 1, 0)
        return (i // nt, i % nt, 0)

    next_w = [w[0].astype(F32) for w in (wq_x, wo_x, w_gate, w_up, w_down)]
    step_rows = D_MODEL // n_tiles
    down_rows = LANES
    n_down_slabs = D_FF // down_rows
    assert step_rows * n_tiles == D_MODEL and step_rows % 16 == 0 and n_down_slabs <= n_tiles
    next_w_specs = [slab_spec(step_rows, D_MODEL, n_tiles), slab_spec(step_rows, D_MODEL, n_tiles),
                    slab_spec(step_rows, D_FF, n_tiles), slab_spec(step_rows, D_FF, n_tiles),
                    slab_spec(down_rows, D_MODEL, n_down_slabs)]

    h1, wq_bf, wo_bf, wg_bf, wu_bf, wd_bf = pl.pallas_call(
        functools.partial(_mixer_kernel, nt, n_down_slabs),
        grid=(n_tiles + 1,),
        in_specs=[pl.BlockSpec(memory_space=pltpu.SMEM),
                  pl.BlockSpec((1, tt, D_MODEL), project_idx),
                  _resident((2, HGRN_WIDTH)), _resident((1, HGRN_HEAD_DIM)),
                  _resident((1, D_MODEL)), _resident((1, D_MODEL)),
                  _resident((D_MODEL, D_IN)), _resident((D_MODEL, D_MODEL))] + next_w_specs,
        out_specs=[pl.BlockSpec((1, tt, D_MODEL), mix_idx)] + next_w_specs,
        out_shape=[tok_shape] + [bf16_like(w) for w in next_w],
        scratch_shapes=_proj_set_shapes(tt) + _proj_set_shapes(tt) + [
            pltpu.VMEM((2, WINDOW, LANES), BF16),
            pltpu.VMEM((2, WINDOW, LANES), BF16),
            pltpu.VMEM((HGRN_HEADS, HGRN_HEAD_DIM, HGRN_HEAD_DIM), F32),
            pltpu.VMEM((tt, D_MODEL), BF16),
        ],
        compiler_params=_params(1),
        name="mixer",
    )(sinks[0].astype(F32), x, hgrn_lb.astype(F32), row(hgrn_onorm[0]), row(g_mix_pre[0]),
      row(g_mix_post[0]), w_in_bf, w_out_bf, *next_w)

    assert T % XFFN_TILE == 0 and XFFN_TILE % (2 * XFFN_PIECE) == 0
    tok_spec = pl.BlockSpec((1, XFFN_TILE, D_MODEL), lambda b, t: (b, t, 0))
    h3 = pl.pallas_call(
        _xattn_ffn_kernel,
        grid=(B, T // XFFN_TILE),
        in_specs=[tok_spec,
                  pl.BlockSpec((1, D_MODEL, MEM_LEN), lambda b, t: (b, 0, 0)),
                  pl.BlockSpec((1, MEM_LEN, D_MODEL), lambda b, t: (b, 0, 0)),
                  _resident((1, D_MODEL)), _resident((1, D_MODEL)),
                  _resident((D_MODEL, D_MODEL)), _resident((D_MODEL, D_MODEL)),
                  _resident((1, D_MODEL)), _resident((1, D_MODEL)),
                  _resident((D_MODEL, D_FF)), _resident((D_MODEL, D_FF)), _resident((D_FF, D_MODEL))],
        out_specs=tok_spec,
        out_shape=tok_shape,
        scratch_shapes=[pltpu.VMEM((XFFN_TILE, D_MODEL), BF16)],
        compiler_params=_params(2),
        name="xattn_ffn",
    )(h1, kt, v, row(g_x_pre[0]), row(g_x_post[0]), wq_bf, wo_bf,
      row(g_ffn_pre[0]), row(g_ffn_post[0]), wg_bf, wu_bf, wd_bf)
    return h3
```

```python
import collections
import functools

import jax
import jax.numpy as jnp
from jax import lax
from jax.experimental import pallas as pl
from jax.experimental.pallas import tpu as pltpu

D_MODEL = 1024
CHUNK = 64
SWA_HEAD_DIM = 64
SWA_HEADS = 8
SWA_KV_HEADS = 2
SWA_WIDTH = SWA_HEADS * SWA_HEAD_DIM
SWA_KV_WIDTH = SWA_KV_HEADS * SWA_HEAD_DIM
WINDOW = 128
HGRN_HEAD_DIM = 128
HGRN_WIDTH = 512
HGRN_HEADS = HGRN_WIDTH // HGRN_HEAD_DIM
D_IN = SWA_WIDTH + 2 * SWA_KV_WIDTH + 4 * HGRN_WIDTH
HGRN_COL0 = SWA_WIDTH + 2 * SWA_KV_WIDTH
MEM_LEN = 256
XATTN_HEADS = 4
XATTN_HEAD_DIM = D_MODEL // XATTN_HEADS
D_FF = 2816
RMS_EPS = 1e-6
NEG_INF = -1e30
LOG2E = 1.4426950408889634

LANES = 128
Q_BLOCK = 128
TOKEN_TILE = 512
VMEM_LIMIT_BYTES = 56 * 1024 * 1024

F32 = jnp.float32
BF16 = jnp.bfloat16


def _rms_scale(x):
    return lax.rsqrt(jnp.mean(x * x, axis=-1, keepdims=True) + RMS_EPS)


def _sigmoid(x):
    return 1.0 / (1.0 + jnp.exp2(x * (-LOG2E)))


def _dot(a, b):
    return jnp.dot(a, b, preferred_element_type=F32)


def _dot_nt(a, b):
    return lax.dot_general(a, b, (((1,), (1,)), ((), ())), preferred_element_type=F32)


def _dot_tn(a, b):
    return lax.dot_general(a, b, (((0,), (0,)), ((), ())), preferred_element_type=F32)


def _cast_slabs(src_refs, dst_refs):
    for src, dst in zip(src_refs, dst_refs):
        dst[...] = src[...].astype(BF16)


def _mem_kv_kernel(mem_ref, g_ref, wk_ref, wv_ref, w_in_ref, w_out_ref,
                   kt_ref, v_ref, w_in_bf_ref, w_out_bf_ref, wk_bf, wv_bf):
    @pl.when(pl.program_id(0) == 0)
    def _():
        _cast_slabs((wk_ref, wv_ref), (wk_bf, wv_bf))

    _cast_slabs((w_in_ref, w_out_ref), (w_in_bf_ref, w_out_bf_ref))
    m = mem_ref[0]
    mn = (m * _rms_scale(m) * g_ref[...]).astype(BF16)
    k = _dot(mn, wk_bf[...])
    kt_ref[0] = jnp.transpose(k).astype(BF16)
    v_ref[0] = _dot(mn, wv_bf[...]).astype(BF16)


ProjSet = collections.namedtuple("ProjSet", "x zq k v qe ke qb kd dec iv sg")


def _proj_set_shapes(tt):
    return [
        pltpu.VMEM((tt, D_MODEL), F32),
        pltpu.VMEM((tt, SWA_WIDTH), BF16),
        pltpu.VMEM((2, WINDOW + tt, LANES), BF16),
        pltpu.VMEM((2, WINDOW + tt, LANES), BF16),
        pltpu.VMEM((tt, HGRN_WIDTH), BF16),
        pltpu.VMEM((tt, HGRN_WIDTH), BF16),
        pltpu.VMEM((tt, HGRN_WIDTH), BF16),
        pltpu.VMEM((tt, HGRN_WIDTH), BF16),
        pltpu.VMEM((tt // CHUNK, HGRN_WIDTH, HGRN_HEAD_DIM), F32),
        pltpu.VMEM((tt, HGRN_WIDTH), BF16),
        pltpu.VMEM((tt, HGRN_WIDTH), F32),
    ]


class _ProjectStage:
    def __init__(self, x_ref, gpre_ref, lbraw_ref, w_in_ref, ps):
        self.x_ref, self.gpre_ref, self.lbraw_ref, self.w_in_ref, self.ps = (
            x_ref, gpre_ref, lbraw_ref, w_in_ref, ps)
        self.tt = x_ref.shape[1]

    def _hgrn_cols(self, i):
        return _dot(self.u, self.w_in_ref[:, HGRN_COL0 + i * HGRN_WIDTH:HGRN_COL0 + (i + 1) * HGRN_WIDTH])

    def norm(self):
        x = self.x_ref[0]
        self.ps.x[...] = x
        self.u = (x * _rms_scale(x) * self.gpre_ref[...]).astype(BF16)

    def swa_q(self):
        zq = _dot(self.u, self.w_in_ref[:, 0:SWA_WIDTH])
        self.ps.zq[...] = (zq * (SWA_HEAD_DIM ** -0.5 * LOG2E)).astype(BF16)

    def swa_kv(self):
        ps = self.ps
        zkv = _dot(self.u, self.w_in_ref[:, SWA_WIDTH:HGRN_COL0])
        k = zkv[:, 0:SWA_KV_WIDTH]
        v = zkv[:, SWA_KV_WIDTH:2 * SWA_KV_WIDTH]
        ps.k[0, WINDOW:, :] = k.astype(BF16)
        ps.k[1, WINDOW:, :] = pltpu.roll(k, SWA_HEAD_DIM, axis=1).astype(BF16)
        ps.v[0, WINDOW:, :] = v.astype(BF16)
        ps.v[1, WINDOW:, :] = pltpu.roll(v, SWA_HEAD_DIM, axis=1).astype(BF16)

    def hgrn_q(self):
        qh = self._hgrn_cols(0)
        self.qf = qh * _sigmoid(qh) * (HGRN_HEAD_DIM ** -0.5)

    def hgrn_f(self):
        a = self.lbraw_ref[...]
        e = jnp.exp(a - jnp.max(a, axis=0, keepdims=True))
        lb = e[0:1] / jnp.sum(e, axis=0, keepdims=True)
        f = lb + (1.0 - lb) * _sigmoid(self._hgrn_cols(1))
        self.kf = 1.0 - f
        self.log2f = jnp.log2(f)

    def hgrn_i(self):
        self.ps.iv[...] = self._hgrn_cols(2).astype(BF16)

    def hgrn_g(self):
        gh = self._hgrn_cols(3)
        self.ps.sg[...] = gh * _sigmoid(gh)

    def hgrn_decay(self):
        ps = self.ps
        cs = 4 * CHUNK
        ri = lax.broadcasted_iota(jnp.int32, (cs, cs), 0)
        ci = lax.broadcasted_iota(jnp.int32, (cs, cs), 1)
        tri = jnp.where((ri // CHUNK == ci // CHUNK) & (ci <= ri), 1.0, 0.0).astype(BF16)
        for rb in range(self.tt // cs):
            lf = self.log2f[rb * cs:(rb + 1) * cs]
            hi = lf.astype(BF16)
            r1 = lf - hi.astype(F32)
            mid = r1.astype(BF16)
            lo = (r1 - mid.astype(F32)).astype(BF16)
            b_blk = _dot(tri, hi) + _dot(tri, mid) + _dot(tri, lo)
            for cc in range(cs // CHUNK):
                c = rb * (cs // CHUNK) + cc
                rows = slice(c * CHUNK, (c + 1) * CHUNK)
                b = b_blk[cc * CHUNK:(cc + 1) * CHUNK]
                b_mid = b[CHUNK // 2 - 1:CHUNK // 2]
                b_last = b[CHUNK - 1:CHUNK]
                qe = self.qf[rows] * jnp.exp2(b - b_mid)
                ke = self.kf[rows] * jnp.exp2(b_mid - b)
                ps.qe[rows, :] = qe.astype(BF16)
                ps.ke[rows, :] = ke.astype(BF16)
                ps.qb[rows, :] = (qe * jnp.exp2(b_mid)).astype(BF16)
                ps.kd[rows, :] = (ke * jnp.exp2(b_last - b_mid)).astype(BF16)
                dec_rows = jnp.broadcast_to(jnp.exp2(b_last), (HGRN_HEAD_DIM, HGRN_WIDTH))
                ps.dec[c] = jnp.transpose(dec_rows)


class _MixStage:
    PAIR_BLOCKS = SWA_WIDTH // LANES
    PAIRS_PER_KV = PAIR_BLOCKS // SWA_KV_HEADS
    N_KEYS = WINDOW + Q_BLOCK

    def __init__(self, first, sinks_ref, onorm_ref, gpost_ref, w_out_ref, ps, kwin, vwin,
                 st_ref, y_ref, out_ref):
        self.first, self.sinks_ref, self.onorm_ref, self.gpost_ref = (
            first, sinks_ref, onorm_ref, gpost_ref)
        self.w_out_ref, self.ps, self.kwin, self.vwin, self.st_ref, self.y_ref, self.out_ref = (
            w_out_ref, ps, kwin, vwin, st_ref, y_ref, out_ref)
        self.tt = out_ref.shape[1]
        self.scores = {}
        self.amat = {}
        self.kv = {}

    def _parity(self, ver, bi):
        return ver ^ (bi // self.PAIRS_PER_KV)

    def swa_window(self):
        ps = self.ps
        zero_win = jnp.zeros((2, WINDOW, LANES), BF16)
        ps.k[:, 0:WINDOW, :] = jnp.where(self.first, zero_win, self.kwin[...])
        ps.v[:, 0:WINDOW, :] = jnp.where(self.first, zero_win, self.vwin[...])
        lane = lax.broadcasted_iota(jnp.int32, (Q_BLOCK, LANES), 1)
        self.low_half = lane < SWA_HEAD_DIM
        row_chunk = lax.broadcasted_iota(jnp.int32, (Q_BLOCK, self.N_KEYS), 0) // CHUNK + WINDOW // CHUNK
        col = lax.broadcasted_iota(jnp.int32, (Q_BLOCK, self.N_KEYS), 1)
        col_chunk = col // CHUNK
        self.band = (col_chunk <= row_chunk) & (col_chunk >= row_chunk - WINDOW // CHUNK)
        self.band_first = self.band & ((col >= WINDOW) | jnp.logical_not(self.first))
        col_q = lax.broadcasted_iota(jnp.int32, (CHUNK, LANES), 1)
        self.keep_low = col_q < CHUNK
        self.keep_high = col_q >= CHUNK

    def _mask_scores(self, j, s):
        if j == 0:
            return jnp.where(self.band_first, s, NEG_INF)
        top = jnp.concatenate([s[0:CHUNK, 0:LANES],
                               jnp.where(self.keep_low, s[0:CHUNK, LANES:2 * LANES], NEG_INF)], axis=1)
        bot = jnp.concatenate([jnp.where(self.keep_high, s[CHUNK:2 * CHUNK, 0:LANES], NEG_INF),
                               s[CHUNK:2 * CHUNK, LANES:2 * LANES]], axis=1)
        return jnp.concatenate([top, bot], axis=0)

    def swa_scores(self, j):
        r0 = j * Q_BLOCK
        qb = self.ps.zq[r0:r0 + Q_BLOCK, :]
        for ver in range(2):
            parts = []
            for bi in range(self.PAIR_BLOCKS):
                keep = self.low_half if self._parity(ver, bi) == 0 else jnp.logical_not(self.low_half)
                parts.append(jnp.where(keep, qb[:, bi * LANES:(bi + 1) * LANES], jnp.zeros((), BF16)))
            qs = jnp.concatenate(parts, axis=0)
            self.scores[j, ver] = _dot_nt(qs, self.ps.k[ver, r0:r0 + self.N_KEYS, :])

    def swa_finish(self, j):
        r0 = j * Q_BLOCK
        halves = [[None, None] for _ in range(self.PAIR_BLOCKS)]
        for ver in range(2):
            s = self.scores.pop((j, ver))
            probs, inv_ls = [], []
            for bi in range(self.PAIR_BLOCKS):
                sink = self.sinks_ref[2 * bi + self._parity(ver, bi)] * LOG2E
                sb = self._mask_scores(j, s[bi * Q_BLOCK:(bi + 1) * Q_BLOCK])
                m = jnp.maximum(jnp.max(sb, axis=-1, keepdims=True), sink)
                p = jnp.exp2(sb - m)
                l = jnp.sum(p, axis=-1, keepdims=True) + jnp.exp2(sink - m)
                probs.append(p.astype(BF16))
                inv_ls.append(1.0 / l)
            o = _dot(jnp.concatenate(probs, axis=0), self.ps.v[ver, r0:r0 + self.N_KEYS, :])
            for bi in range(self.PAIR_BLOCKS):
                halves[bi][self._parity(ver, bi)] = o[bi * Q_BLOCK:(bi + 1) * Q_BLOCK] * inv_ls[bi]
        for bi in range(self.PAIR_BLOCKS):
            blk = jnp.where(self.low_half, halves[bi][0], halves[bi][1])
            self.y_ref[r0:r0 + Q_BLOCK, bi * LANES:(bi + 1) * LANES] = blk.astype(BF16)

    def swa_save_window(self):
        self.kwin[...] = self.ps.k[:, self.tt:self.tt + WINDOW, :]
        self.vwin[...] = self.ps.v[:, self.tt:self.tt + WINDOW, :]

    @staticmethod
    def _chunk(hd, c):
        return slice(c * CHUNK, (c + 1) * CHUNK), slice(hd * HGRN_HEAD_DIM, (hd + 1) * HGRN_HEAD_DIM)

    def hgrn_local(self, pair):
        ps = self.ps
        ri = lax.broadcasted_iota(jnp.int32, (CHUNK, CHUNK), 0)
        ci = lax.broadcasted_iota(jnp.int32, (CHUNK, CHUNK), 1)
        causal = ci <= ri
        zero = jnp.zeros((CHUNK, HGRN_HEAD_DIM), BF16)
        hd_a, hd_b = 2 * pair, 2 * pair + 1
        for c in range(self.tt // CHUNK):
            rows, cols_a = self._chunk(hd_a, c)
            _, cols_b = self._chunk(hd_b, c)
            for hd, cols in ((hd_a, cols_a), (hd_b, cols_b)):
                a = _dot_nt(ps.qe[rows, cols], ps.ke[rows, cols])
                self.amat[hd, c] = jnp.where(causal, a, 0.0).astype(BF16)
            kd_ab = jnp.concatenate([ps.kd[rows, cols_a], ps.kd[rows, cols_b]], axis=0)
            iv_ab = jnp.concatenate([jnp.concatenate([ps.iv[rows, cols_a], zero], axis=1),
                                     jnp.concatenate([zero, ps.iv[rows, cols_b]], axis=1)], axis=0)
            kv = _dot_tn(kd_ab, iv_ab)
            self.kv[hd_a, c] = kv[:, 0:HGRN_HEAD_DIM]
            self.kv[hd_b, c] = kv[:, HGRN_HEAD_DIM:2 * HGRN_HEAD_DIM]

    def hgrn_state(self, hd):
        ps = self.ps
        onorm = self.onorm_ref[...]
        st = self.st_ref[hd] * jnp.where(self.first, 0.0, 1.0)
        for c in range(self.tt // CHUNK):
            rows, cols = self._chunk(hd, c)
            lhs = jnp.concatenate([ps.qb[rows, cols], self.amat.pop((hd, c))], axis=1)
            rhs = jnp.concatenate([st.astype(BF16), ps.iv[rows, cols]], axis=0)
            o = _dot(lhs, rhs)
            st = ps.dec[c, cols, :] * st + self.kv.pop((hd, c))
            on = o * _rms_scale(o) * onorm
            self.y_ref[rows, SWA_WIDTH + hd * HGRN_HEAD_DIM:SWA_WIDTH + (hd + 1) * HGRN_HEAD_DIM] = (
                on * self.ps.sg[rows, cols]).astype(BF16)
        self.st_ref[hd] = st

    def out_swa(self):
        self.y = _dot(self.y_ref[:, 0:SWA_WIDTH], self.w_out_ref[0:SWA_WIDTH, :])

    def out_hgrn(self):
        y = self.y + _dot(self.y_ref[:, SWA_WIDTH:D_MODEL], self.w_out_ref[SWA_WIDTH:D_MODEL, :])
        self.out_ref[0] = self.ps.x[...] + y * _rms_scale(y) * self.gpost_ref[...]


def _mixer_kernel(tiles_per_seq, n_down_slabs, sinks_ref, xa_ref, lbraw_ref, onorm_ref, gpre_ref,
                  gpost_ref, w_in_ref, w_out_ref, wq_ref, wo_ref, wg_ref, wu_ref, wd_ref,
                  out_ref, wq_bf_ref, wo_bf_ref, wg_bf_ref, wu_bf_ref, wd_bf_ref, *scratch):
    n_set = len(ProjSet._fields)
    sets = (ProjSet(*scratch[0:n_set]), ProjSet(*scratch[n_set:2 * n_set]))
    kwin, vwin, st_ref, y_ref = scratch[2 * n_set:]
    s = pl.program_id(0)

    _cast_slabs((wq_ref, wo_ref, wg_ref, wu_ref), (wq_bf_ref, wo_bf_ref, wg_bf_ref, wu_bf_ref))

    @pl.when(s < n_down_slabs)
    def _():
        _cast_slabs((wd_ref,), (wd_bf_ref,))

    @pl.when(s == 0)
    def _():
        for ref in sets[1] + (kwin, vwin, st_ref):
            ref[...] = jnp.zeros_like(ref)

    first = lax.rem(jnp.maximum(s - 1, 0), tiles_per_seq) == 0

    def body(write_set, read_set):
        pj = _ProjectStage(xa_ref, gpre_ref, lbraw_ref, w_in_ref, write_set)
        mx = _MixStage(first, sinks_ref, onorm_ref, gpost_ref, w_out_ref, read_set, kwin, vwin,
                       st_ref, y_ref, out_ref)
        mx.swa_window()
        pj.norm()
        mx.swa_scores(0)
        pj.hgrn_q()
        mx.swa_finish(0)
        mx.swa_scores(1)
        pj.hgrn_f()
        mx.swa_finish(1)
        mx.swa_scores(2)
        pj.hgrn_g()
        mx.swa_finish(2)
        mx.swa_scores(3)
        pj.hgrn_decay()
        mx.swa_finish(3)
        mx.swa_save_window()
        mx.hgrn_local(0)
        mx.hgrn_local(1)
        pj.swa_q()
        mx.hgrn_state(0)
        mx.hgrn_state(1)
        pj.swa_kv()
        mx.hgrn_state(2)
        mx.out_swa()
        mx.hgrn_state(3)
        mx.out_hgrn()
        pj.hgrn_i()

    @pl.when(lax.rem(s, 2) == 0)
    def _():
        body(sets[0], sets[1])

    @pl.when(lax.rem(s, 2) == 1)
    def _():
        body(sets[1], sets[0])


def _xattn_ffn_kernel(h_ref, kt_ref, v_ref, gxpre_ref, gxpost_ref, wq_ref, wo_ref,
                      gfpre_ref, gfpost_ref, wg_ref, wu_ref, wd_ref, out_ref, o_ref):
    half = h_ref.shape[1] // 2
    halves = (slice(0, half), slice(half, 2 * half))
    head_cols = [slice(hd * XATTN_HEAD_DIM, (hd + 1) * XATTN_HEAD_DIM) for hd in range(XATTN_HEADS)]

    def queries(rows):
        h = h_ref[0, rows, :]
        u = (h * _rms_scale(h) * gxpre_ref[...]).astype(BF16)
        return (_dot(u, wq_ref[...]) * (XATTN_HEAD_DIM ** -0.5 * LOG2E)).astype(BF16)

    def scores(q):
        return [_dot(q[:, cols], kt_ref[0, cols, :]) for cols in head_cols]

    def attend(rows, s_heads):
        for cols, s in zip(head_cols, s_heads):
            p = jnp.exp2(s - jnp.max(s, axis=-1, keepdims=True))
            l = jnp.sum(p, axis=-1, keepdims=True)
            o = _dot(p.astype(BF16), v_ref[0, :, cols]) * (1.0 / l)
            o_ref[rows, cols] = o.astype(BF16)

    def attn_out(rows):
        return _dot(o_ref[rows, :], wo_ref[...])

    def attn_residual(rows, y):
        return h_ref[0, rows, :] + y * _rms_scale(y) * gxpost_ref[...]

    def ffn_in(h2):
        return (h2 * _rms_scale(h2) * gfpre_ref[...]).astype(BF16)

    def ffn_residual(rows, h2, y):
        out_ref[0, rows, :] = h2 + y * _rms_scale(y) * gfpost_ref[...]

    q_a = queries(halves[0])
    s_a = scores(q_a)
    q_b = queries(halves[1])
    attend(halves[0], s_a)
    s_b = scores(q_b)
    y_a = attn_out(halves[0])
    attend(halves[1], s_b)
    h2_a = attn_residual(halves[0], y_a)
    y_b = attn_out(halves[1])
    u_a = ffn_in(h2_a)
    gate_a = _dot(u_a, wg_ref[...])
    up_a = _dot(u_a, wu_ref[...])
    h2_b = attn_residual(halves[1], y_b)
    u_b = ffn_in(h2_b)
    act_a = (gate_a * _sigmoid(gate_a) * up_a).astype(BF16)
    gate_b = _dot(u_b, wg_ref[...])
    z_a = _dot(act_a, wd_ref[...])
    up_b = _dot(u_b, wu_ref[...])
    ffn_residual(halves[0], h2_a, z_a)
    act_b = (gate_b * _sigmoid(gate_b) * up_b).astype(BF16)
    z_b = _dot(act_b, wd_ref[...])
    ffn_residual(halves[1], h2_b, z_b)


def _resident(shape):
    return pl.BlockSpec(shape, lambda *_: (0,) * len(shape), pipeline_mode=pl.Buffered(1))


def _params(n_axes):
    return pltpu.CompilerParams(dimension_semantics=("arbitrary",) * n_axes,
                                vmem_limit_bytes=VMEM_LIMIT_BYTES)


def kernel(x, mem, w_in, sinks, hgrn_lb, hgrn_onorm, w_out, g_mix_pre, g_mix_post, g_mem, g_x_pre,
           g_x_post, wq_x, wk_x, wv_x, wo_x, g_ffn_pre, g_ffn_post, w_gate, w_up, w_down):
    B, T, D = x.shape
    assert D == D_MODEL and T % TOKEN_TILE == 0 and mem.shape == (B, MEM_LEN, D_MODEL)
    assert w_in.shape == (1, D_MODEL, D_IN) and hgrn_lb.shape[0] == 2
    tt = TOKEN_TILE
    nt = T // tt
    n_tiles = B * nt

    def row(g):
        return g.reshape(1, -1).astype(F32)

    tok_spec = pl.BlockSpec((1, tt, D_MODEL), lambda b, t: (b, t, 0))
    tok_shape = jax.ShapeDtypeStruct((B, T, D_MODEL), F32)

    def slab_spec(rows, cols, n_slabs):
        return pl.BlockSpec((rows, cols), lambda s: (jnp.minimum(s, n_slabs - 1), 0))

    def bf16_like(w):
        return jax.ShapeDtypeStruct(w.shape, BF16)

    w_in_f, w_out_f = w_in[0].astype(F32), w_out[0].astype(F32)
    in_rows = D_MODEL // B
    kt, v, w_in_bf, w_out_bf = pl.pallas_call(
        _mem_kv_kernel,
        grid=(B,),
        in_specs=[pl.BlockSpec((1, MEM_LEN, D_MODEL), lambda b: (b, 0, 0)),
                  _resident((1, D_MODEL)), _resident((D_MODEL, D_MODEL)), _resident((D_MODEL, D_MODEL)),
                  slab_spec(in_rows, D_IN, B), slab_spec(in_rows, D_MODEL, B)],
        out_specs=[pl.BlockSpec((1, D_MODEL, MEM_LEN), lambda b: (b, 0, 0)),
                   pl.BlockSpec((1, MEM_LEN, D_MODEL), lambda b: (b, 0, 0)),
                   slab_spec(in_rows, D_IN, B), slab_spec(in_rows, D_MODEL, B)],
        out_shape=[jax.ShapeDtypeStruct((B, D_MODEL, MEM_LEN), BF16),
                   jax.ShapeDtypeStruct((B, MEM_LEN, D_MODEL), BF16),
                   bf16_like(w_in_f), bf16_like(w_out_f)],
        scratch_shapes=[pltpu.VMEM((D_MODEL, D_MODEL), BF16), pltpu.VMEM((D_MODEL, D_MODEL), BF16)],
        compiler_params=_params(1),
        name="mem_kv",
    )(mem, row(g_mem[0]), wk_x[0].astype(F32), wv_x[0].astype(F32), w_in_f, w_out_f)

    def project_idx(s):
        i = jnp.minimum(s, n_tiles - 1)
        return (i // nt, i % nt, 0)

    def mix_idx(s):
        i = jnp.maximum(s - 1, 0)
        return (i // nt, i % nt, 0)

    next_w = [w[0].astype(F32) for w in (wq_x, wo_x, w_gate, w_up, w_down)]
    step_rows = D_MODEL // n_tiles
    down_rows = LANES
    n_down_slabs = D_FF // down_rows
    assert step_rows * n_tiles == D_MODEL and step_rows % 16 == 0 and n_down_slabs <= n_tiles
    next_w_specs = [slab_spec(step_rows, D_MODEL, n_tiles), slab_spec(step_rows, D_MODEL, n_tiles),
                    slab_spec(step_rows, D_FF, n_tiles), slab_spec(step_rows, D_FF, n_tiles),
                    slab_spec(down_rows, D_MODEL, n_down_slabs)]

    h1, wq_bf, wo_bf, wg_bf, wu_bf, wd_bf = pl.pallas_call(
        functools.partial(_mixer_kernel, nt, n_down_slabs),
        grid=(n_tiles + 1,),
        in_specs=[pl.BlockSpec(memory_space=pltpu.SMEM),
                  pl.BlockSpec((1, tt, D_MODEL), project_idx),
                  _resident((2, HGRN_WIDTH)), _resident((1, HGRN_HEAD_DIM)),
                  _resident((1, D_MODEL)), _resident((1, D_MODEL)),
                  _resident((D_MODEL, D_IN)), _resident((D_MODEL, D_MODEL))] + next_w_specs,
        out_specs=[pl.BlockSpec((1, tt, D_MODEL), mix_idx)] + next_w_specs,
        out_shape=[tok_shape] + [bf16_like(w) for w in next_w],
        scratch_shapes=_proj_set_shapes(tt) + _proj_set_shapes(tt) + [
            pltpu.VMEM((2, WINDOW, LANES), BF16),
            pltpu.VMEM((2, WINDOW, LANES), BF16),
            pltpu.VMEM((HGRN_HEADS, HGRN_HEAD_DIM, HGRN_HEAD_DIM), F32),
            pltpu.VMEM((tt, D_MODEL), BF16),
        ],
        compiler_params=_params(1),
        name="mixer",
    )(sinks[0].astype(F32), x, hgrn_lb.astype(F32), row(hgrn_onorm[0]), row(g_mix_pre[0]),
      row(g_mix_post[0]), w_in_bf, w_out_bf, *next_w)

    h3 = pl.pallas_call(
        _xattn_ffn_kernel,
        grid=(B, nt),
        in_specs=[tok_spec,
                  pl.BlockSpec((1, D_MODEL, MEM_LEN), lambda b, t: (b, 0, 0)),
                  pl.BlockSpec((1, MEM_LEN, D_MODEL), lambda b, t: (b, 0, 0)),
                  _resident((1, D_MODEL)), _resident((1, D_MODEL)),
                  _resident((D_MODEL, D_MODEL)), _resident((D_MODEL, D_MODEL)),
                  _resident((1, D_MODEL)), _resident((1, D_MODEL)),
                  _resident((D_MODEL, D_FF)), _resident((D_MODEL, D_FF)), _resident((D_FF, D_MODEL))],
        out_specs=tok_spec,
        out_shape=tok_shape,
        scratch_shapes=[pltpu.VMEM((tt, D_MODEL), BF16)],
        compiler_params=_params(2),
        name="xattn_ffn",
    )(h1, kt, v, row(g_x_pre[0]), row(g_x_post[0]), wq_bf, wo_bf,
      row(g_ffn_pre[0]), row(g_ffn_post[0]), wg_bf, wu_bf, wd_bf)
    return h3
```

```python
import collections
import functools

import jax
import jax.numpy as jnp
from jax import lax
from jax.experimental import pallas as pl
from jax.experimental.pallas import tpu as pltpu

D_MODEL = 1024
CHUNK = 64
SWA_HEAD_DIM = 64
SWA_HEADS = 8
SWA_KV_HEADS = 2
SWA_WIDTH = SWA_HEADS * SWA_HEAD_DIM
SWA_KV_WIDTH = SWA_KV_HEADS * SWA_HEAD_DIM
WINDOW = 128
HGRN_HEAD_DIM = 128
HGRN_WIDTH = 512
HGRN_HEADS = HGRN_WIDTH // HGRN_HEAD_DIM
D_IN = SWA_WIDTH + 2 * SWA_KV_WIDTH + 4 * HGRN_WIDTH
HGRN_COL0 = SWA_WIDTH + 2 * SWA_KV_WIDTH
MEM_LEN = 256
XATTN_HEADS = 4
XATTN_HEAD_DIM = D_MODEL // XATTN_HEADS
D_FF = 2816
RMS_EPS = 1e-6
NEG_INF = -1e30
LOG2E = 1.4426950408889634

LANES = 128
Q_BLOCK = 128
TOKEN_TILE = 512
VMEM_LIMIT_BYTES = 56 * 1024 * 1024

F32 = jnp.float32
BF16 = jnp.bfloat16


def _rms_scale(x):
    return lax.rsqrt(jnp.mean(x * x, axis=-1, keepdims=True) + RMS_EPS)


def _sigmoid(x):
    return 1.0 / (1.0 + jnp.exp2(x * (-LOG2E)))


def _dot(a, b):
    return jnp.dot(a, b, preferred_element_type=F32)


def _dot_nt(a, b):
    return lax.dot_general(a, b, (((1,), (1,)), ((), ())), preferred_element_type=F32)


def _dot_tn(a, b):
    return lax.dot_general(a, b, (((0,), (0,)), ((), ())), preferred_element_type=F32)


def _cast_slabs(src_refs, dst_refs):
    for src, dst in zip(src_refs, dst_refs):
        dst[...] = src[...].astype(BF16)


def _mem_kv_kernel(mem_ref, g_ref, wk_ref, wv_ref, w_in_ref, w_out_ref,
                   kt_ref, v_ref, w_in_bf_ref, w_out_bf_ref, wk_bf, wv_bf):
    @pl.when(pl.program_id(0) == 0)
    def _():
        _cast_slabs((wk_ref, wv_ref), (wk_bf, wv_bf))

    _cast_slabs((w_in_ref, w_out_ref), (w_in_bf_ref, w_out_bf_ref))
    m = mem_ref[0]
    mn = (m * _rms_scale(m) * g_ref[...]).astype(BF16)
    k = _dot(mn, wk_bf[...])
    kt_ref[0] = jnp.transpose(k).astype(BF16)
    v_ref[0] = _dot(mn, wv_bf[...]).astype(BF16)


ProjSet = collections.namedtuple("ProjSet", "x zq k v qe ke qb kd dec iv sg")


def _proj_set_shapes(tt):
    return [
        pltpu.VMEM((tt, D_MODEL), F32),
        pltpu.VMEM((tt, SWA_WIDTH), BF16),
        pltpu.VMEM((2, WINDOW + tt, LANES), BF16),
        pltpu.VMEM((2, WINDOW + tt, LANES), BF16),
        pltpu.VMEM((tt, HGRN_WIDTH), BF16),
        pltpu.VMEM((tt, HGRN_WIDTH), BF16),
        pltpu.VMEM((tt, HGRN_WIDTH), BF16),
        pltpu.VMEM((tt, HGRN_WIDTH), BF16),
        pltpu.VMEM((tt // CHUNK, HGRN_WIDTH, HGRN_HEAD_DIM), F32),
        pltpu.VMEM((tt, HGRN_WIDTH), BF16),
        pltpu.VMEM((tt, HGRN_WIDTH), F32),
    ]


class _ProjectStage:
    def __init__(self, x_ref, gpre_ref, lbraw_ref, w_in_ref, ps):
        self.x_ref, self.gpre_ref, self.lbraw_ref, self.w_in_ref, self.ps = (
            x_ref, gpre_ref, lbraw_ref, w_in_ref, ps)
        self.tt = x_ref.shape[1]

    def _hgrn_cols(self, i):
        return _dot(self.u, self.w_in_ref[:, HGRN_COL0 + i * HGRN_WIDTH:HGRN_COL0 + (i + 1) * HGRN_WIDTH])

    def norm(self):
        x = self.x_ref[0]
        self.ps.x[...] = x
        self.u = (x * _rms_scale(x) * self.gpre_ref[...]).astype(BF16)

    def swa_q(self):
        zq = _dot(self.u, self.w_in_ref[:, 0:SWA_WIDTH])
        self.ps.zq[...] = (zq * (SWA_HEAD_DIM ** -0.5 * LOG2E)).astype(BF16)

    def swa_kv(self):
        ps = self.ps
        zkv = _dot(self.u, self.w_in_ref[:, SWA_WIDTH:HGRN_COL0])
        k = zkv[:, 0:SWA_KV_WIDTH]
        v = zkv[:, SWA_KV_WIDTH:2 * SWA_KV_WIDTH]
        ps.k[0, WINDOW:, :] = k.astype(BF16)
        ps.k[1, WINDOW:, :] = pltpu.roll(k, SWA_HEAD_DIM, axis=1).astype(BF16)
        ps.v[0, WINDOW:, :] = v.astype(BF16)
        ps.v[1, WINDOW:, :] = pltpu.roll(v, SWA_HEAD_DIM, axis=1).astype(BF16)

    def hgrn_q(self):
        qh = self._hgrn_cols(0)
        self.qf = qh * _sigmoid(qh) * (HGRN_HEAD_DIM ** -0.5)

    def hgrn_f(self):
        a = self.lbraw_ref[...]
        e = jnp.exp(a - jnp.max(a, axis=0, keepdims=True))
        lb = e[0:1] / jnp.sum(e, axis=0, keepdims=True)
        f = lb + (1.0 - lb) * _sigmoid(self._hgrn_cols(1))
        self.kf = 1.0 - f
        self.log2f = jnp.log2(f)

    def hgrn_i(self):
        self.ps.iv[...] = self._hgrn_cols(2).astype(BF16)

    def hgrn_g(self):
        gh = self._hgrn_cols(3)
        self.ps.sg[...] = gh * _sigmoid(gh)

    def hgrn_decay(self):
        ps = self.ps
        cs = 4 * CHUNK
        ri = lax.broadcasted_iota(jnp.int32, (cs, cs), 0)
        ci = lax.broadcasted_iota(jnp.int32, (cs, cs), 1)
        tri = jnp.where((ri // CHUNK == ci // CHUNK) & (ci <= ri), 1.0, 0.0).astype(BF16)
        for rb in range(self.tt // cs):
            lf = self.log2f[rb * cs:(rb + 1) * cs]
            hi = lf.astype(BF16)
            r1 = lf - hi.astype(F32)
            mid = r1.astype(BF16)
            lo = (r1 - mid.astype(F32)).astype(BF16)
            b_blk = _dot(tri, hi) + _dot(tri, mid) + _dot(tri, lo)
            for cc in range(cs // CHUNK):
                c = rb * (cs // CHUNK) + cc
                rows = slice(c * CHUNK, (c + 1) * CHUNK)
                b = b_blk[cc * CHUNK:(cc + 1) * CHUNK]
                b_mid = b[CHUNK // 2 - 1:CHUNK // 2]
                b_last = b[CHUNK - 1:CHUNK]
                qe = self.qf[rows] * jnp.exp2(b - b_mid)
                ke = self.kf[rows] * jnp.exp2(b_mid - b)
                ps.qe[rows, :] = qe.astype(BF16)
                ps.ke[rows, :] = ke.astype(BF16)
                ps.qb[rows, :] = (qe * jnp.exp2(b_mid)).astype(BF16)
                ps.kd[rows, :] = (ke * jnp.exp2(b_last - b_mid)).astype(BF16)
                dec_rows = jnp.broadcast_to(jnp.exp2(b_last), (HGRN_HEAD_DIM, HGRN_WIDTH))
                ps.dec[c] = jnp.transpose(dec_rows)


class _MixStage:
    PAIR_BLOCKS = SWA_WIDTH // LANES
    PAIRS_PER_KV = PAIR_BLOCKS // SWA_KV_HEADS
    N_KEYS = WINDOW + Q_BLOCK

    def __init__(self, first, sinks_ref, onorm_ref, gpost_ref, w_out_ref, ps, kwin, vwin,
                 st_ref, y_ref, out_ref):
        self.first, self.sinks_ref, self.onorm_ref, self.gpost_ref = (
            first, sinks_ref, onorm_ref, gpost_ref)
        self.w_out_ref, self.ps, self.kwin, self.vwin, self.st_ref, self.y_ref, self.out_ref = (
            w_out_ref, ps, kwin, vwin, st_ref, y_ref, out_ref)
        self.tt = out_ref.shape[1]
        self.scores = {}
        self.amat = {}
        self.kv = {}

    def _parity(self, ver, bi):
        return ver ^ (bi // self.PAIRS_PER_KV)

    def swa_window(self):
        ps = self.ps
        zero_win = jnp.zeros((2, WINDOW, LANES), BF16)
        ps.k[:, 0:WINDOW, :] = jnp.where(self.first, zero_win, self.kwin[...])
        ps.v[:, 0:WINDOW, :] = jnp.where(self.first, zero_win, self.vwin[...])
        lane = lax.broadcasted_iota(jnp.int32, (Q_BLOCK, LANES), 1)
        self.low_half = lane < SWA_HEAD_DIM
        row_chunk = lax.broadcasted_iota(jnp.int32, (Q_BLOCK, self.N_KEYS), 0) // CHUNK + WINDOW // CHUNK
        col = lax.broadcasted_iota(jnp.int32, (Q_BLOCK, self.N_KEYS), 1)
        col_chunk = col // CHUNK
        self.band = (col_chunk <= row_chunk) & (col_chunk >= row_chunk - WINDOW // CHUNK)
        self.band_first = self.band & ((col >= WINDOW) | jnp.logical_not(self.first))
        col_q = lax.broadcasted_iota(jnp.int32, (CHUNK, LANES), 1)
        self.keep_low = col_q < CHUNK
        self.keep_high = col_q >= CHUNK

    def _mask_scores(self, j, s):
        if j == 0:
            return jnp.where(self.band_first, s, NEG_INF)
        top = jnp.concatenate([s[0:CHUNK, 0:LANES],
                               jnp.where(self.keep_low, s[0:CHUNK, LANES:2 * LANES], NEG_INF)], axis=1)
        bot = jnp.concatenate([jnp.where(self.keep_high, s[CHUNK:2 * CHUNK, 0:LANES], NEG_INF),
                               s[CHUNK:2 * CHUNK, LANES:2 * LANES]], axis=1)
        return jnp.concatenate([top, bot], axis=0)

    def swa_scores(self, j):
        r0 = j * Q_BLOCK
        qb = self.ps.zq[r0:r0 + Q_BLOCK, :]
        for ver in range(2):
            parts = []
            for bi in range(self.PAIR_BLOCKS):
                keep = self.low_half if self._parity(ver, bi) == 0 else jnp.logical_not(self.low_half)
                parts.append(jnp.where(keep, qb[:, bi * LANES:(bi + 1) * LANES], jnp.zeros((), BF16)))
            qs = jnp.concatenate(parts, axis=0)
            self.scores[j, ver] = _dot_nt(qs, self.ps.k[ver, r0:r0 + self.N_KEYS, :])

    def swa_finish(self, j):
        r0 = j * Q_BLOCK
        halves = [[None, None] for _ in range(self.PAIR_BLOCKS)]
        for ver in range(2):
            s = self.scores.pop((j, ver))
            probs, inv_ls = [], []
            for bi in range(self.PAIR_BLOCKS):
                sink = self.sinks_ref[2 * bi + self._parity(ver, bi)] * LOG2E
                sb = self._mask_scores(j, s[bi * Q_BLOCK:(bi + 1) * Q_BLOCK])
                m = jnp.maximum(jnp.max(sb, axis=-1, keepdims=True), sink)
                p = jnp.exp2(sb - m)
                l = jnp.sum(p, axis=-1, keepdims=True) + jnp.exp2(sink - m)
                probs.append(p.astype(BF16))
                inv_ls.append(1.0 / l)
            o = _dot(jnp.concatenate(probs, axis=0), self.ps.v[ver, r0:r0 + self.N_KEYS, :])
            for bi in range(self.PAIR_BLOCKS):
                halves[bi][self._parity(ver, bi)] = o[bi * Q_BLOCK:(bi + 1) * Q_BLOCK] * inv_ls[bi]
        for bi in range(self.PAIR_BLOCKS):
            blk = jnp.where(self.low_half, halves[bi][0], halves[bi][1])
            self.y_ref[r0:r0 + Q_BLOCK, bi * LANES:(bi + 1) * LANES] = blk.astype(BF16)

    def swa_save_window(self):
        self.kwin[...] = self.ps.k[:, self.tt:self.tt + WINDOW, :]
        self.vwin[...] = self.ps.v[:, self.tt:self.tt + WINDOW, :]

    @staticmethod
    def _chunk(hd, c):
        return slice(c * CHUNK, (c + 1) * CHUNK), slice(hd * HGRN_HEAD_DIM, (hd + 1) * HGRN_HEAD_DIM)

    def hgrn_local(self, pair):
        ps = self.ps
        ri = lax.broadcasted_iota(jnp.int32, (CHUNK, CHUNK), 0)
        ci = lax.broadcasted_iota(jnp.int32, (CHUNK, CHUNK), 1)
        causal = ci <= ri
        zero = jnp.zeros((CHUNK, HGRN_HEAD_DIM), BF16)
        hd_a, hd_b = 2 * pair, 2 * pair + 1
        for c in range(self.tt // CHUNK):
            rows, cols_a = self._chunk(hd_a, c)
            _, cols_b = self._chunk(hd_b, c)
            for hd, cols in ((hd_a, cols_a), (hd_b, cols_b)):
                a = _dot_nt(ps.qe[rows, cols], ps.ke[rows, cols])
                self.amat[hd, c] = jnp.where(causal, a, 0.0).astype(BF16)
            kd_ab = jnp.concatenate([ps.kd[rows, cols_a], ps.kd[rows, cols_b]], axis=0)
            iv_ab = jnp.concatenate([jnp.concatenate([ps.iv[rows, cols_a], zero], axis=1),
                                     jnp.concatenate([zero, ps.iv[rows, cols_b]], axis=1)], axis=0)
            kv = _dot_tn(kd_ab, iv_ab)
            self.kv[hd_a, c] = kv[:, 0:HGRN_HEAD_DIM]
            self.kv[hd_b, c] = kv[:, HGRN_HEAD_DIM:2 * HGRN_HEAD_DIM]

    def hgrn_state(self, hd):
        ps = self.ps
        onorm = self.onorm_ref[...]
        st = self.st_ref[hd] * jnp.where(self.first, 0.0, 1.0)
        for c in range(self.tt // CHUNK):
            rows, cols = self._chunk(hd, c)
            lhs = jnp.concatenate([ps.qb[rows, cols], self.amat.pop((hd, c))], axis=1)
            rhs = jnp.concatenate([st.astype(BF16), ps.iv[rows, cols]], axis=0)
            o = _dot(lhs, rhs)
            st = ps.dec[c, cols, :] * st + self.kv.pop((hd, c))
            on = o * _rms_scale(o) * onorm
            self.y_ref[rows, SWA_WIDTH + hd * HGRN_HEAD_DIM:SWA_WIDTH + (hd + 1) * HGRN_HEAD_DIM] = (
                on * self.ps.sg[rows, cols]).astype(BF16)
        self.st_ref[hd] = st

    def out_swa(self):
        self.y = _dot(self.y_ref[:, 0:SWA_WIDTH], self.w_out_ref[0:SWA_WIDTH, :])

    def out_hgrn(self):
        y = self.y + _dot(self.y_ref[:, SWA_WIDTH:D_MODEL], self.w_out_ref[SWA_WIDTH:D_MODEL, :])
        self.out_ref[0] = self.ps.x[...] + y * _rms_scale(y) * self.gpost_ref[...]


def _mixer_kernel(tiles_per_seq, n_down_slabs, sinks_ref, xa_ref, lbraw_ref, onorm_ref, gpre_ref,
                  gpost_ref, w_in_ref, w_out_ref, wq_ref, wo_ref, wg_ref, wu_ref, wd_ref,
                  out_ref, wq_bf_ref, wo_bf_ref, wg_bf_ref, wu_bf_ref, wd_bf_ref, *scratch):
    n_set = len(ProjSet._fields)
    sets = (ProjSet(*scratch[0:n_set]), ProjSet(*scratch[n_set:2 * n_set]))
    kwin, vwin, st_ref, y_ref = scratch[2 * n_set:]
    s = pl.program_id(0)

    _cast_slabs((wq_ref, wo_ref, wg_ref, wu_ref), (wq_bf_ref, wo_bf_ref, wg_bf_ref, wu_bf_ref))

    @pl.when(s < n_down_slabs)
    def _():
        _cast_slabs((wd_ref,), (wd_bf_ref,))

    @pl.when(s == 0)
    def _():
        for ref in sets[1] + (kwin, vwin, st_ref):
            ref[...] = jnp.zeros_like(ref)

    first = lax.rem(jnp.maximum(s - 1, 0), tiles_per_seq) == 0

    def body(write_set, read_set):
        pj = _ProjectStage(xa_ref, gpre_ref, lbraw_ref, w_in_ref, write_set)
        mx = _MixStage(first, sinks_ref, onorm_ref, gpost_ref, w_out_ref, read_set, kwin, vwin,
                       st_ref, y_ref, out_ref)
        mx.swa_window()
        pj.norm()
        mx.swa_scores(0)
        pj.hgrn_q()
        mx.swa_finish(0)
        mx.swa_scores(2)
        mx.swa_scores(1)
        pj.hgrn_f()
        mx.swa_finish(1)
        pj.hgrn_g()
        mx.swa_scores(3)
        mx.swa_finish(2)
        pj.hgrn_decay()
        mx.swa_finish(3)
        mx.swa_save_window()
        mx.hgrn_local(0)
        mx.hgrn_local(1)
        pj.swa_q()
        mx.hgrn_state(0)
        mx.hgrn_state(1)
        pj.swa_kv()
        mx.hgrn_state(3)
        mx.hgrn_state(2)
        mx.out_swa()
        mx.out_hgrn()
        pj.hgrn_i()

    @pl.when(lax.rem(s, 2) == 0)
    def _():
        body(sets[0], sets[1])

    @pl.when(lax.rem(s, 2) == 1)
    def _():
        body(sets[1], sets[0])


def _xattn_ffn_kernel(h_ref, kt_ref, v_ref, gxpre_ref, gxpost_ref, wq_ref, wo_ref,
                      gfpre_ref, gfpost_ref, wg_ref, wu_ref, wd_ref, out_ref, o_ref):
    half = h_ref.shape[1] // 2
    halves = (slice(0, half), slice(half, 2 * half))
    head_cols = [slice(hd * XATTN_HEAD_DIM, (hd + 1) * XATTN_HEAD_DIM) for hd in range(XATTN_HEADS)]

    def queries(rows):
        h = h_ref[0, rows, :]
        u = (h * _rms_scale(h) * gxpre_ref[...]).astype(BF16)
        return (_dot(u, wq_ref[...]) * (XATTN_HEAD_DIM ** -0.5 * LOG2E)).astype(BF16)

    def scores(q):
        return [_dot(q[:, cols], kt_ref[0, cols, :]) for cols in head_cols]

    def attend(rows, s_heads):
        for cols, s in zip(head_cols, s_heads):
            p = jnp.exp2(s - jnp.max(s, axis=-1, keepdims=True))
            l = jnp.sum(p, axis=-1, keepdims=True)
            o = _dot(p.astype(BF16), v_ref[0, :, cols]) * (1.0 / l)
            o_ref[rows, cols] = o.astype(BF16)

    def attn_out(rows):
        return _dot(o_ref[rows, :], wo_ref[...])

    def attn_residual(rows, y):
        return h_ref[0, rows, :] + y * _rms_scale(y) * gxpost_ref[...]

    def ffn_in(h2):
        return (h2 * _rms_scale(h2) * gfpre_ref[...]).astype(BF16)

    def ffn_residual(rows, h2, y):
        out_ref[0, rows, :] = h2 + y * _rms_scale(y) * gfpost_ref[...]

    q_a = queries(halves[0])
    s_a = scores(q_a)
    q_b = queries(halves[1])
    attend(halves[0], s_a)
    s_b = scores(q_b)
    y_a = attn_out(halves[0])
    attend(halves[1], s_b)
    h2_a = attn_residual(halves[0], y_a)
    y_b = attn_out(halves[1])
    u_a = ffn_in(h2_a)
    up_a = _dot(u_a, wu_ref[...])
    h2_b = attn_residual(halves[1], y_b)
    gate_a = _dot(u_a, wg_ref[...])
    u_b = ffn_in(h2_b)
    act_a = (gate_a * _sigmoid(gate_a) * up_a).astype(BF16)
    z_a = _dot(act_a, wd_ref[...])
    up_b = _dot(u_b, wu_ref[...])
    gate_b = _dot(u_b, wg_ref[...])
    ffn_residual(halves[0], h2_a, z_a)
    act_b = (gate_b * _sigmoid(gate_b) * up_b).astype(BF16)
    z_b = _dot(act_b, wd_ref[...])
    ffn_residual(halves[1], h2_b, z_b)


def _resident(shape):
    return pl.BlockSpec(shape, lambda *_: (0,) * len(shape), pipeline_mode=pl.Buffered(1))


def _params(n_axes):
    return pltpu.CompilerParams(dimension_semantics=("arbitrary",) * n_axes,
                                vmem_limit_bytes=VMEM_LIMIT_BYTES)


def kernel(x, mem, w_in, sinks, hgrn_lb, hgrn_onorm, w_out, g_mix_pre, g_mix_post, g_mem, g_x_pre,
           g_x_post, wq_x, wk_x, wv_x, wo_x, g_ffn_pre, g_ffn_post, w_gate, w_up, w_down):
    B, T, D = x.shape
    assert D == D_MODEL and T % TOKEN_TILE == 0 and mem.shape == (B, MEM_LEN, D_MODEL)
    assert w_in.shape == (1, D_MODEL, D_IN) and hgrn_lb.shape[0] == 2
    tt = TOKEN_TILE
    nt = T // tt
    n_tiles = B * nt

    def row(g):
        return g.reshape(1, -1).astype(F32)

    tok_spec = pl.BlockSpec((1, tt, D_MODEL), lambda b, t: (b, t, 0))
    tok_shape = jax.ShapeDtypeStruct((B, T, D_MODEL), F32)

    def slab_spec(rows, cols, n_slabs):
        return pl.BlockSpec((rows, cols), lambda s: (jnp.minimum(s, n_slabs - 1), 0))

    def bf16_like(w):
        return jax.ShapeDtypeStruct(w.shape, BF16)

    w_in_f, w_out_f = w_in[0].astype(F32), w_out[0].astype(F32)
    in_rows = D_MODEL // B
    kt, v, w_in_bf, w_out_bf = pl.pallas_call(
        _mem_kv_kernel,
        grid=(B,),
        in_specs=[pl.BlockSpec((1, MEM_LEN, D_MODEL), lambda b: (b, 0, 0)),
                  _resident((1, D_MODEL)), _resident((D_MODEL, D_MODEL)), _resident((D_MODEL, D_MODEL)),
                  slab_spec(in_rows, D_IN, B), slab_spec(in_rows, D_MODEL, B)],
        out_specs=[pl.BlockSpec((1, D_MODEL, MEM_LEN), lambda b: (b, 0, 0)),
                   pl.BlockSpec((1, MEM_LEN, D_MODEL), lambda b: (b, 0, 0)),
                   slab_spec(in_rows, D_IN, B), slab_spec(in_rows, D_MODEL, B)],
        out_shape=[jax.ShapeDtypeStruct((B, D_MODEL, MEM_LEN), BF16),
                   jax.ShapeDtypeStruct((B, MEM_LEN, D_MODEL), BF16),
                   bf16_like(w_in_f), bf16_like(w_out_f)],
        scratch_shapes=[pltpu.VMEM((D_MODEL, D_MODEL), BF16), pltpu.VMEM((D_MODEL, D_MODEL), BF16)],
        compiler_params=_params(1),
        name="mem_kv",
    )(mem, row(g_mem[0]), wk_x[0].astype(F32), wv_x[0].astype(F32), w_in_f, w_out_f)

    def project_idx(s):
        i = jnp.minimum(s, n_tiles - 1)
        return (i // nt, i % nt, 0)

    def mix_idx(s):
        i = jnp.maximum(s - 1, 0)
        return (i // nt, i % nt, 0)

    next_w = [w[0].astype(F32) for w in (wq_x, wo_x, w_gate, w_up, w_down)]
    step_rows = D_MODEL // n_tiles
    down_rows = LANES
    n_down_slabs = D_FF // down_rows
    assert step_rows * n_tiles == D_MODEL and step_rows % 16 == 0 and n_down_slabs <= n_tiles
    next_w_specs = [slab_spec(step_rows, D_MODEL, n_tiles), slab_spec(step_rows, D_MODEL, n_tiles),
                    slab_spec(step_rows, D_FF, n_tiles), slab_spec(step_rows, D_FF, n_tiles),
                    slab_spec(down_rows, D_MODEL, n_down_slabs)]

    h1, wq_bf, wo_bf, wg_bf, wu_bf, wd_bf = pl.pallas_call(
        functools.partial(_mixer_kernel, nt, n_down_slabs),
        grid=(n_tiles + 1,),
        in_specs=[pl.BlockSpec(memory_space=pltpu.SMEM),
                  pl.BlockSpec((1, tt, D_MODEL), project_idx),
                  _resident((2, HGRN_WIDTH)), _resident((1, HGRN_HEAD_DIM)),
                  _resident((1, D_MODEL)), _resident((1, D_MODEL)),
                  _resident((D_MODEL, D_IN)), _resident((D_MODEL, D_MODEL))] + next_w_specs,
        out_specs=[pl.BlockSpec((1, tt, D_MODEL), mix_idx)] + next_w_specs,
        out_shape=[tok_shape] + [bf16_like(w) for w in next_w],
        scratch_shapes=_proj_set_shapes(tt) + _proj_set_shapes(tt) + [
            pltpu.VMEM((2, WINDOW, LANES), BF16),
            pltpu.VMEM((2, WINDOW, LANES), BF16),
            pltpu.VMEM((HGRN_HEADS, HGRN_HEAD_DIM, HGRN_HEAD_DIM), F32),
            pltpu.VMEM((tt, D_MODEL), BF16),
        ],
        compiler_params=_params(1),
        name="mixer",
    )(sinks[0].astype(F32), x, hgrn_lb.astype(F32), row(hgrn_onorm[0]), row(g_mix_pre[0]),
      row(g_mix_post[0]), w_in_bf, w_out_bf, *next_w)

    h3 = pl.pallas_call(
        _xattn_ffn_kernel,
        grid=(B, nt),
        in_specs=[tok_spec,
                  pl.BlockSpec((1, D_MODEL, MEM_LEN), lambda b, t: (b, 0, 0)),
                  pl.BlockSpec((1, MEM_LEN, D_MODEL), lambda b, t: (b, 0, 0)),
                  _resident((1, D_MODEL)), _resident((1, D_MODEL)),
                  _resident((D_MODEL, D_MODEL)), _resident((D_MODEL, D_MODEL)),
                  _resident((1, D_MODEL)), _resident((1, D_MODEL)),
                  _resident((D_MODEL, D_FF)), _resident((D_MODEL, D_FF)), _resident((D_FF, D_MODEL))],
        out_specs=tok_spec,
        out_shape=tok_shape,
        scratch_shapes=[pltpu.VMEM((tt, D_MODEL), BF16)],
        compiler_params=_params(2),
        name="xattn_ffn",
    )(h1, kt, v, row(g_x_pre[0]), row(g_x_post[0]), wq_bf, wo_bf,
      row(g_ffn_pre[0]), row(g_ffn_post[0]), wg_bf, wu_bf, wd_bf)
    return h3
```

```python
import collections
import functools

import jax
import jax.numpy as jnp
from jax import lax
from jax.experimental import pallas as pl
from jax.experimental.pallas import tpu as pltpu

D_MODEL = 1024
CHUNK = 64
SWA_HEAD_DIM = 64
SWA_HEADS = 8
SWA_KV_HEADS = 2
SWA_WIDTH = SWA_HEADS * SWA_HEAD_DIM
SWA_KV_WIDTH = SWA_KV_HEADS * SWA_HEAD_DIM
WINDOW = 128
HGRN_HEAD_DIM = 128
HGRN_WIDTH = 512
HGRN_HEADS = HGRN_WIDTH // HGRN_HEAD_DIM
D_IN = SWA_WIDTH + 2 * SWA_KV_WIDTH + 4 * HGRN_WIDTH
HGRN_COL0 = SWA_WIDTH + 2 * SWA_KV_WIDTH
MEM_LEN = 256
XATTN_HEADS = 4
XATTN_HEAD_DIM = D_MODEL // XATTN_HEADS
D_FF = 2816
RMS_EPS = 1e-6
NEG_INF = -1e30
LOG2E = 1.4426950408889634

LANES = 128
Q_BLOCK = 128
TOKEN_TILE = 512
VMEM_LIMIT_BYTES = 56 * 1024 * 1024

F32 = jnp.float32
BF16 = jnp.bfloat16


def _rms_scale(x):
    return lax.rsqrt(jnp.mean(x * x, axis=-1, keepdims=True) + RMS_EPS)


def _sigmoid(x):
    return 1.0 / (1.0 + jnp.exp2(x * (-LOG2E)))


def _dot(a, b):
    return jnp.dot(a, b, preferred_element_type=F32)


def _dot_nt(a, b):
    return lax.dot_general(a, b, (((1,), (1,)), ((), ())), preferred_element_type=F32)


def _dot_tn(a, b):
    return lax.dot_general(a, b, (((0,), (0,)), ((), ())), preferred_element_type=F32)


def _cast_slabs(src_refs, dst_refs):
    for src, dst in zip(src_refs, dst_refs):
        dst[...] = src[...].astype(BF16)


def _mem_kv_kernel(mem_ref, g_ref, wk_ref, wv_ref, w_in_ref, w_out_ref,
                   kt_ref, v_ref, w_in_bf_ref, w_out_bf_ref, wk_bf, wv_bf):
    @pl.when(pl.program_id(0) == 0)
    def _():
        _cast_slabs((wk_ref, wv_ref), (wk_bf, wv_bf))

    _cast_slabs((w_in_ref, w_out_ref), (w_in_bf_ref, w_out_bf_ref))
    m = mem_ref[0]
    mn = (m * _rms_scale(m) * g_ref[...]).astype(BF16)
    k = _dot(mn, wk_bf[...])
    kt_ref[0] = jnp.transpose(k).astype(BF16)
    v_ref[0] = _dot(mn, wv_bf[...]).astype(BF16)


ProjSet = collections.namedtuple("ProjSet", "x zq k v qe ke qb kd dec iv sg")


def _proj_set_shapes(tt):
    return [
        pltpu.VMEM((tt, D_MODEL), F32),
        pltpu.VMEM((tt, SWA_WIDTH), BF16),
        pltpu.VMEM((2, WINDOW + tt, LANES), BF16),
        pltpu.VMEM((2, WINDOW + tt, LANES), BF16),
        pltpu.VMEM((tt, HGRN_WIDTH), BF16),
        pltpu.VMEM((tt, HGRN_WIDTH), BF16),
        pltpu.VMEM((tt, HGRN_WIDTH), BF16),
        pltpu.VMEM((tt, HGRN_WIDTH), BF16),
        pltpu.VMEM((tt // CHUNK, HGRN_WIDTH, HGRN_HEAD_DIM), F32),
        pltpu.VMEM((tt, HGRN_WIDTH), BF16),
        pltpu.VMEM((tt, HGRN_WIDTH), F32),
    ]


class _ProjectStage:
    def __init__(self, x_ref, gpre_ref, lbraw_ref, w_in_ref, ps):
        self.x_ref, self.gpre_ref, self.lbraw_ref, self.w_in_ref, self.ps = (
            x_ref, gpre_ref, lbraw_ref, w_in_ref, ps)
        self.tt = x_ref.shape[1]

    def _hgrn_cols(self, i):
        return _dot(self.u, self.w_in_ref[:, HGRN_COL0 + i * HGRN_WIDTH:HGRN_COL0 + (i + 1) * HGRN_WIDTH])

    def norm(self):
        x = self.x_ref[0]
        self.ps.x[...] = x
        self.u = (x * _rms_scale(x) * self.gpre_ref[...]).astype(BF16)

    def swa_q(self):
        zq = _dot(self.u, self.w_in_ref[:, 0:SWA_WIDTH])
        self.ps.zq[...] = (zq * (SWA_HEAD_DIM ** -0.5 * LOG2E)).astype(BF16)

    def swa_kv(self):
        ps = self.ps
        zkv = _dot(self.u, self.w_in_ref[:, SWA_WIDTH:HGRN_COL0])
        k = zkv[:, 0:SWA_KV_WIDTH]
        v = zkv[:, SWA_KV_WIDTH:2 * SWA_KV_WIDTH]
        ps.k[0, WINDOW:, :] = k.astype(BF16)
        ps.k[1, WINDOW:, :] = pltpu.roll(k, SWA_HEAD_DIM, axis=1).astype(BF16)
        ps.v[0, WINDOW:, :] = v.astype(BF16)
        ps.v[1, WINDOW:, :] = pltpu.roll(v, SWA_HEAD_DIM, axis=1).astype(BF16)

    def hgrn_q(self):
        qh = self._hgrn_cols(0)
        self.qf = qh * _sigmoid(qh) * (HGRN_HEAD_DIM ** -0.5)

    def hgrn_f(self):
        a = self.lbraw_ref[...]
        e = jnp.exp(a - jnp.max(a, axis=0, keepdims=True))
        lb = e[0:1] / jnp.sum(e, axis=0, keepdims=True)
        f = lb + (1.0 - lb) * _sigmoid(self._hgrn_cols(1))
        self.kf = 1.0 - f
        self.log2f = jnp.log2(f)

    def hgrn_i(self):
        self.ps.iv[...] = self._hgrn_cols(2).astype(BF16)

    def hgrn_g(self):
        gh = self._hgrn_cols(3)
        self.ps.sg[...] = gh * _sigmoid(gh)

    def hgrn_decay(self):
        ps = self.ps
        cs = 4 * CHUNK
        ri = lax.broadcasted_iota(jnp.int32, (cs, cs), 0)
        ci = lax.broadcasted_iota(jnp.int32, (cs, cs), 1)
        tri = jnp.where((ri // CHUNK == ci // CHUNK) & (ci <= ri), 1.0, 0.0).astype(BF16)
        for rb in range(self.tt // cs):
            lf = self.log2f[rb * cs:(rb + 1) * cs]
            hi = lf.astype(BF16)
            lo = (lf - hi.astype(F32)).astype(BF16)
            b_blk = _dot(tri, hi) + _dot(tri, lo)
            for cc in range(cs // CHUNK):
                c = rb * (cs // CHUNK) + cc
                rows = slice(c * CHUNK, (c + 1) * CHUNK)
                b = b_blk[cc * CHUNK:(cc + 1) * CHUNK]
                b_mid = b[CHUNK // 2 - 1:CHUNK // 2]
                b_last = b[CHUNK - 1:CHUNK]
                qe = self.qf[rows] * jnp.exp2(b - b_mid)
                ke = self.kf[rows] * jnp.exp2(b_mid - b)
                ps.qe[rows, :] = qe.astype(BF16)
                ps.ke[rows, :] = ke.astype(BF16)
                ps.qb[rows, :] = (qe * jnp.exp2(b_mid)).astype(BF16)
                ps.kd[rows, :] = (ke * jnp.exp2(b_last - b_mid)).astype(BF16)
                dec_rows = jnp.broadcast_to(jnp.exp2(b_last), (HGRN_HEAD_DIM, HGRN_WIDTH))
                ps.dec[c] = jnp.transpose(dec_rows)


class _MixStage:
    PAIR_BLOCKS = SWA_WIDTH // LANES
    PAIRS_PER_KV = PAIR_BLOCKS // SWA_KV_HEADS
    N_KEYS = WINDOW + Q_BLOCK

    def __init__(self, first, sinks_ref, onorm_ref, gpost_ref, w_out_ref, ps, kwin, vwin,
                 st_ref, y_ref, out_ref):
        self.first, self.sinks_ref, self.onorm_ref, self.gpost_ref = (
            first, sinks_ref, onorm_ref, gpost_ref)
        self.w_out_ref, self.ps, self.kwin, self.vwin, self.st_ref, self.y_ref, self.out_ref = (
            w_out_ref, ps, kwin, vwin, st_ref, y_ref, out_ref)
        self.tt = out_ref.shape[1]
        self.scores = {}
        self.amat = {}
        self.kv = {}

    def _parity(self, ver, bi):
        return ver ^ (bi // self.PAIRS_PER_KV)

    def swa_window(self):
        ps = self.ps
        zero_win = jnp.zeros((2, WINDOW, LANES), BF16)
        ps.k[:, 0:WINDOW, :] = jnp.where(self.first, zero_win, self.kwin[...])
        ps.v[:, 0:WINDOW, :] = jnp.where(self.first, zero_win, self.vwin[...])
        lane = lax.broadcasted_iota(jnp.int32, (Q_BLOCK, LANES), 1)
        self.low_half = lane < SWA_HEAD_DIM
        row_chunk = lax.broadcasted_iota(jnp.int32, (Q_BLOCK, self.N_KEYS), 0) // CHUNK + WINDOW // CHUNK
        col = lax.broadcasted_iota(jnp.int32, (Q_BLOCK, self.N_KEYS), 1)
        col_chunk = col // CHUNK
        self.band = (col_chunk <= row_chunk) & (col_chunk >= row_chunk - WINDOW // CHUNK)
        self.band_first = self.band & ((col >= WINDOW) | jnp.logical_not(self.first))
        col_q = lax.broadcasted_iota(jnp.int32, (CHUNK, LANES), 1)
        self.keep_low = col_q < CHUNK
        self.keep_high = col_q >= CHUNK

    def _mask_scores(self, j, s):
        if j == 0:
            return jnp.where(self.band_first, s, NEG_INF)
        top = jnp.concatenate([s[0:CHUNK, 0:LANES],
                               jnp.where(self.keep_low, s[0:CHUNK, LANES:2 * LANES], NEG_INF)], axis=1)
        bot = jnp.concatenate([jnp.where(self.keep_high, s[CHUNK:2 * CHUNK, 0:LANES], NEG_INF),
                               s[CHUNK:2 * CHUNK, LANES:2 * LANES]], axis=1)
        return jnp.concatenate([top, bot], axis=0)

    def swa_scores(self, j):
        r0 = j * Q_BLOCK
        qb = self.ps.zq[r0:r0 + Q_BLOCK, :]
        for ver in range(2):
            parts = []
            for bi in range(self.PAIR_BLOCKS):
                keep = self.low_half if self._parity(ver, bi) == 0 else jnp.logical_not(self.low_half)
                parts.append(jnp.where(keep, qb[:, bi * LANES:(bi + 1) * LANES], jnp.zeros((), BF16)))
            qs = jnp.concatenate(parts, axis=0)
            self.scores[j, ver] = _dot_nt(qs, self.ps.k[ver, r0:r0 + self.N_KEYS, :])

    def swa_finish(self, j):
        r0 = j * Q_BLOCK
        halves = [[None, None] for _ in range(self.PAIR_BLOCKS)]
        for ver in range(2):
            s = self.scores.pop((j, ver))
            probs, inv_ls = [], []
            for bi in range(self.PAIR_BLOCKS):
                sink = self.sinks_ref[2 * bi + self._parity(ver, bi)] * LOG2E
                sb = self._mask_scores(j, s[bi * Q_BLOCK:(bi + 1) * Q_BLOCK])
                m = jnp.maximum(jnp.max(sb, axis=-1, keepdims=True), sink)
                p = jnp.exp2(sb - m)
                l = jnp.sum(p, axis=-1, keepdims=True) + jnp.exp2(sink - m)
                probs.append(p.astype(BF16))
                inv_ls.append(1.0 / l)
            o = _dot(jnp.concatenate(probs, axis=0), self.ps.v[ver, r0:r0 + self.N_KEYS, :])
            for bi in range(self.PAIR_BLOCKS):
                halves[bi][self._parity(ver, bi)] = o[bi * Q_BLOCK:(bi + 1) * Q_BLOCK] * inv_ls[bi]
        for bi in range(self.PAIR_BLOCKS):
            blk = jnp.where(self.low_half, halves[bi][0], halves[bi][1])
            self.y_ref[r0:r0 + Q_BLOCK, bi * LANES:(bi + 1) * LANES] = blk.astype(BF16)

    def swa_save_window(self):
        self.kwin[...] = self.ps.k[:, self.tt:self.tt + WINDOW, :]
        self.vwin[...] = self.ps.v[:, self.tt:self.tt + WINDOW, :]

    @staticmethod
    def _chunk(hd, c):
        return slice(c * CHUNK, (c + 1) * CHUNK), slice(hd * HGRN_HEAD_DIM, (hd + 1) * HGRN_HEAD_DIM)

    def hgrn_local(self, pair):
        ps = self.ps
        ri = lax.broadcasted_iota(jnp.int32, (CHUNK, CHUNK), 0)
        ci = lax.broadcasted_iota(jnp.int32, (CHUNK, CHUNK), 1)
        causal = ci <= ri
        hd_a, hd_b = 2 * pair, 2 * pair + 1
        for c in range(self.tt // CHUNK):
            rows, cols_a = self._chunk(hd_a, c)
            _, cols_b = self._chunk(hd_b, c)
            for hd, cols in ((hd_a, cols_a), (hd_b, cols_b)):
                a = _dot_nt(ps.qe[rows, cols], ps.ke[rows, cols])
                self.amat[hd, c] = jnp.where(causal, a, 0.0).astype(BF16)
            for hd, cols in ((hd_a, cols_a), (hd_b, cols_b)):
                self.kv[hd, c] = _dot_tn(ps.kd[rows, cols], ps.iv[rows, cols])

    def hgrn_state(self, hd):
        ps = self.ps
        onorm = self.onorm_ref[...]
        st = self.st_ref[hd] * jnp.where(self.first, 0.0, 1.0)
        for c in range(self.tt // CHUNK):
            rows, cols = self._chunk(hd, c)
            lhs = jnp.concatenate([ps.qb[rows, cols], self.amat.pop((hd, c))], axis=1)
            rhs = jnp.concatenate([st.astype(BF16), ps.iv[rows, cols]], axis=0)
            o = _dot(lhs, rhs)
            st = ps.dec[c, cols, :] * st + self.kv.pop((hd, c))
            on = o * _rms_scale(o) * onorm
            self.y_ref[rows, SWA_WIDTH + hd * HGRN_HEAD_DIM:SWA_WIDTH + (hd + 1) * HGRN_HEAD_DIM] = (
                on * self.ps.sg[rows, cols]).astype(BF16)
        self.st_ref[hd] = st

    def out_swa(self):
        self.y = _dot(self.y_ref[:, 0:SWA_WIDTH], self.w_out_ref[0:SWA_WIDTH, :])

    def out_hgrn(self):
        y = self.y + _dot(self.y_ref[:, SWA_WIDTH:D_MODEL], self.w_out_ref[SWA_WIDTH:D_MODEL, :])
        self.out_ref[0] = self.ps.x[...] + y * _rms_scale(y) * self.gpost_ref[...]


def _mixer_kernel(tiles_per_seq, n_down_slabs, sinks_ref, xa_ref, lbraw_ref, onorm_ref, gpre_ref,
                  gpost_ref, w_in_ref, w_out_ref, wq_ref, wo_ref, wg_ref, wu_ref, wd_ref,
                  out_ref, wq_bf_ref, wo_bf_ref, wg_bf_ref, wu_bf_ref, wd_bf_ref, *scratch):
    n_set = len(ProjSet._fields)
    sets = (ProjSet(*scratch[0:n_set]), ProjSet(*scratch[n_set:2 * n_set]))
    kwin, vwin, st_ref, y_ref = scratch[2 * n_set:]
    s = pl.program_id(0)

    _cast_slabs((wq_ref, wo_ref, wg_ref, wu_ref), (wq_bf_ref, wo_bf_ref, wg_bf_ref, wu_bf_ref))

    @pl.when(s < n_down_slabs)
    def _():
        _cast_slabs((wd_ref,), (wd_bf_ref,))

    @pl.when(s == 0)
    def _():
        for ref in sets[1] + (kwin, vwin, st_ref):
            ref[...] = jnp.zeros_like(ref)

    first = lax.rem(jnp.maximum(s - 1, 0), tiles_per_seq) == 0

    def body(write_set, read_set):
        pj = _ProjectStage(xa_ref, gpre_ref, lbraw_ref, w_in_ref, write_set)
        mx = _MixStage(first, sinks_ref, onorm_ref, gpost_ref, w_out_ref, read_set, kwin, vwin,
                       st_ref, y_ref, out_ref)
        mx.swa_window()
        pj.norm()
        mx.swa_scores(0)
        pj.hgrn_q()
        mx.swa_finish(0)
        mx.swa_scores(2)
        mx.swa_scores(1)
        pj.hgrn_f()
        mx.swa_finish(1)
        pj.hgrn_g()
        mx.swa_scores(3)
        mx.swa_finish(2)
        pj.hgrn_decay()
        mx.swa_finish(3)
        mx.swa_save_window()
        mx.hgrn_local(0)
        mx.hgrn_local(1)
        pj.swa_q()
        mx.hgrn_state(0)
        mx.hgrn_state(1)
        pj.swa_kv()
        mx.hgrn_state(3)
        mx.hgrn_state(2)
        mx.out_swa()
        mx.out_hgrn()
        pj.hgrn_i()

    @pl.when(lax.rem(s, 2) == 0)
    def _():
        body(sets[0], sets[1])

    @pl.when(lax.rem(s, 2) == 1)
    def _():
        body(sets[1], sets[0])


def _xattn_ffn_kernel(h_ref, kt_ref, v_ref, gxpre_ref, gxpost_ref, wq_ref, wo_ref,
                      gfpre_ref, gfpost_ref, wg_ref, wu_ref, wd_ref, out_ref, o_ref):
    half = h_ref.shape[1] // 2
    halves = (slice(0, half), slice(half, 2 * half))
    head_cols = [slice(hd * XATTN_HEAD_DIM, (hd + 1) * XATTN_HEAD_DIM) for hd in range(XATTN_HEADS)]

    def queries(rows):
        h = h_ref[0, rows, :]
        u = (h * _rms_scale(h) * gxpre_ref[...]).astype(BF16)
        return (_dot(u, wq_ref[...]) * (XATTN_HEAD_DIM ** -0.5 * LOG2E)).astype(BF16)

    def scores(q):
        return [_dot(q[:, cols], kt_ref[0, cols, :]) for cols in head_cols]

    def attend(rows, s_heads):
        for cols, s in zip(head_cols, s_heads):
            p = jnp.exp2(s - jnp.max(s, axis=-1, keepdims=True))
            l = jnp.sum(p, axis=-1, keepdims=True)
            o = _dot(p.astype(BF16), v_ref[0, :, cols]) * (1.0 / l)
            o_ref[rows, cols] = o.astype(BF16)

    def attn_out(rows):
        return _dot(o_ref[rows, :], wo_ref[...])

    def attn_residual(rows, y):
        return h_ref[0, rows, :] + y * _rms_scale(y) * gxpost_ref[...]

    def ffn_in(h2):
        return (h2 * _rms_scale(h2) * gfpre_ref[...]).astype(BF16)

    def ffn_residual(rows, h2, y):
        out_ref[0, rows, :] = h2 + y * _rms_scale(y) * gfpost_ref[...]

    q_a = queries(halves[0])
    s_a = scores(q_a)
    q_b = queries(halves[1])
    attend(halves[0], s_a)
    s_b = scores(q_b)
    attend(halves[1], s_b)
    y_a = attn_out(halves[0])
    h2_a = attn_residual(halves[0], y_a)
    y_b = attn_out(halves[1])
    u_a = ffn_in(h2_a)
    up_a = _dot(u_a, wu_ref[...])
    h2_b = attn_residual(halves[1], y_b)
    gate_a = _dot(u_a, wg_ref[...])
    u_b = ffn_in(h2_b)
    act_a = (gate_a * _sigmoid(gate_a) * up_a).astype(BF16)
    z_a = _dot(act_a, wd_ref[...])
    up_b = _dot(u_b, wu_ref[...])
    gate_b = _dot(u_b, wg_ref[...])
    ffn_residual(halves[0], h2_a, z_a)
    act_b = (gate_b * _sigmoid(gate_b) * up_b).astype(BF16)
    z_b = _dot(act_b, wd_ref[...])
    ffn_residual(halves[1], h2_b, z_b)


def _resident(shape):
    return pl.BlockSpec(shape, lambda *_: (0,) * len(shape), pipeline_mode=pl.Buffered(1))


def _params(n_axes):
    return pltpu.CompilerParams(dimension_semantics=("arbitrary",) * n_axes,
                                vmem_limit_bytes=VMEM_LIMIT_BYTES)


def kernel(x, mem, w_in, sinks, hgrn_lb, hgrn_onorm, w_out, g_mix_pre, g_mix_post, g_mem, g_x_pre,
           g_x_post, wq_x, wk_x, wv_x, wo_x, g_ffn_pre, g_ffn_post, w_gate, w_up, w_down):
    B, T, D = x.shape
    assert D == D_MODEL and T % TOKEN_TILE == 0 and mem.shape == (B, MEM_LEN, D_MODEL)
    assert w_in.shape == (1, D_MODEL, D_IN) and hgrn_lb.shape[0] == 2
    tt = TOKEN_TILE
    nt = T // tt
    n_tiles = B * nt

    def row(g):
        return g.reshape(1, -1).astype(F32)

    tok_spec = pl.BlockSpec((1, tt, D_MODEL), lambda b, t: (b, t, 0))
    tok_shape = jax.ShapeDtypeStruct((B, T, D_MODEL), F32)

    def slab_spec(rows, cols, n_slabs):
        return pl.BlockSpec((rows, cols), lambda s: (jnp.minimum(s, n_slabs - 1), 0))

    def bf16_like(w):
        return jax.ShapeDtypeStruct(w.shape, BF16)

    w_in_f, w_out_f = w_in[0].astype(F32), w_out[0].astype(F32)
    in_rows = D_MODEL // B
    kt, v, w_in_bf, w_out_bf = pl.pallas_call(
        _mem_kv_kernel,
        grid=(B,),
        in_specs=[pl.BlockSpec((1, MEM_LEN, D_MODEL), lambda b: (b, 0, 0)),
                  _resident((1, D_MODEL)), _resident((D_MODEL, D_MODEL)), _resident((D_MODEL, D_MODEL)),
                  slab_spec(in_rows, D_IN, B), slab_spec(in_rows, D_MODEL, B)],
        out_specs=[pl.BlockSpec((1, D_MODEL, MEM_LEN), lambda b: (b, 0, 0)),
                   pl.BlockSpec((1, MEM_LEN, D_MODEL), lambda b: (b, 0, 0)),
                   slab_spec(in_rows, D_IN, B), slab_spec(in_rows, D_MODEL, B)],
        out_shape=[jax.ShapeDtypeStruct((B, D_MODEL, MEM_LEN), BF16),
                   jax.ShapeDtypeStruct((B, MEM_LEN, D_MODEL), BF16),
                   bf16_like(w_in_f), bf16_like(w_out_f)],
        scratch_shapes=[pltpu.VMEM((D_MODEL, D_MODEL), BF16), pltpu.VMEM((D_MODEL, D_MODEL), BF16)],
        compiler_params=_params(1),
        name="mem_kv",
    )(mem, row(g_mem[0]), wk_x[0].astype(F32), wv_x[0].astype(F32), w_in_f, w_out_f)

    def project_idx(s):
        i = jnp.minimum(s, n_tiles - 1)
        return (i // nt, i % nt, 0)

    def mix_idx(s):
        i = jnp.maximum(s - 1, 0)
        return (i // nt, i % nt, 0)

    next_w = [w[0].astype(F32) for w in (wq_x, wo_x, w_gate, w_up, w_down)]
    step_rows = D_MODEL // n_tiles
    down_rows = LANES
    n_down_slabs = D_FF // down_rows
    assert step_rows * n_tiles == D_MODEL and step_rows % 16 == 0 and n_down_slabs <= n_tiles
    next_w_specs = [slab_spec(step_rows, D_MODEL, n_tiles), slab_spec(step_rows, D_MODEL, n_tiles),
                    slab_spec(step_rows, D_FF, n_tiles), slab_spec(step_rows, D_FF, n_tiles),
                    slab_spec(down_rows, D_MODEL, n_down_slabs)]

    h1, wq_bf, wo_bf, wg_bf, wu_bf, wd_bf = pl.pallas_call(
        functools.partial(_mixer_kernel, nt, n_down_slabs),
        grid=(n_tiles + 1,),
        in_specs=[pl.BlockSpec(memory_space=pltpu.SMEM),
                  pl.BlockSpec((1, tt, D_MODEL), project_idx),
                  _resident((2, HGRN_WIDTH)), _resident((1, HGRN_HEAD_DIM)),
                  _resident((1, D_MODEL)), _resident((1, D_MODEL)),
                  _resident((D_MODEL, D_IN)), _resident((D_MODEL, D_MODEL))] + next_w_specs,
        out_specs=[pl.BlockSpec((1, tt, D_MODEL), mix_idx)] + next_w_specs,
        out_shape=[tok_shape] + [bf16_like(w) for w in next_w],
        scratch_shapes=_proj_set_shapes(tt) + _proj_set_shapes(tt) + [
            pltpu.VMEM((2, WINDOW, LANES), BF16),
            pltpu.VMEM((2, WINDOW, LANES), BF16),
            pltpu.VMEM((HGRN_HEADS, HGRN_HEAD_DIM, HGRN_HEAD_DIM), F32),
            pltpu.VMEM((tt, D_MODEL), BF16),
        ],
        compiler_params=_params(1),
        name="mixer",
    )(sinks[0].astype(F32), x, hgrn_lb.astype(F32), row(hgrn_onorm[0]), row(g_mix_pre[0]),
      row(g_mix_post[0]), w_in_bf, w_out_bf, *next_w)

    h3 = pl.pallas_call(
        _xattn_ffn_kernel,
        grid=(B, nt),
        in_specs=[tok_spec,
                  pl.BlockSpec((1, D_MODEL, MEM_LEN), lambda b, t: (b, 0, 0)),
                  pl.BlockSpec((1, MEM_LEN, D_MODEL), lambda b, t: (b, 0, 0)),
                  _resident((1, D_MODEL)), _resident((1, D_MODEL)),
                  _resident((D_MODEL, D_MODEL)), _resident((D_MODEL, D_MODEL)),
                  _resident((1, D_MODEL)), _resident((1, D_MODEL)),
                  _resident((D_MODEL, D_FF)), _resident((D_MODEL, D_FF)), _resident((D_FF, D_MODEL))],
        out_specs=tok_spec,
        out_shape=tok_shape,
        scratch_shapes=[pltpu.VMEM((tt, D_MODEL), BF16)],
        compiler_params=_params(2),
        name="xattn_ffn",
    )(h1, kt, v, row(g_x_pre[0]), row(g_x_post[0]), wq_bf, wo_bf,
      row(g_ffn_pre[0]), row(g_ffn_post[0]), wg_bf, wu_bf, wd_bf)
    return h3
```

```python
import collections
import functools

import jax
import jax.numpy as jnp
from jax import lax
from jax.experimental import pallas as pl
from jax.experimental.pallas import tpu as pltpu

D_MODEL = 1024
CHUNK = 64
SWA_HEAD_DIM = 64
SWA_HEADS = 8
SWA_KV_HEADS = 2
SWA_WIDTH = SWA_HEADS * SWA_HEAD_DIM
SWA_KV_WIDTH = SWA_KV_HEADS * SWA_HEAD_DIM
WINDOW = 128
HGRN_HEAD_DIM = 128
HGRN_WIDTH = 512
HGRN_HEADS = HGRN_WIDTH // HGRN_HEAD_DIM
D_IN = SWA_WIDTH + 2 * SWA_KV_WIDTH + 4 * HGRN_WIDTH
HGRN_COL0 = SWA_WIDTH + 2 * SWA_KV_WIDTH
MEM_LEN = 256
XATTN_HEADS = 4
XATTN_HEAD_DIM = D_MODEL // XATTN_HEADS
D_FF = 2816
RMS_EPS = 1e-6
NEG_INF = -1e30
LOG2E = 1.4426950408889634

LANES = 128
Q_BLOCK = 128
TOKEN_TILE = 512
VMEM_LIMIT_BYTES = 56 * 1024 * 1024

F32 = jnp.float32
BF16 = jnp.bfloat16


def _rms_scale(x):
    return lax.rsqrt(jnp.mean(x * x, axis=-1, keepdims=True) + RMS_EPS)


def _sigmoid(x):
    return 1.0 / (1.0 + jnp.exp2(x * (-LOG2E)))


def _dot(a, b):
    return jnp.dot(a, b, preferred_element_type=F32)


def _dot_nt(a, b):
    return lax.dot_general(a, b, (((1,), (1,)), ((), ())), preferred_element_type=F32)


def _dot_tn(a, b):
    return lax.dot_general(a, b, (((0,), (0,)), ((), ())), preferred_element_type=F32)


def _cast_slabs(src_refs, dst_refs):
    for src, dst in zip(src_refs, dst_refs):
        dst[...] = src[...].astype(BF16)


def _mem_kv_kernel(mem_ref, g_ref, wk_ref, wv_ref, w_in_ref, w_out_ref,
                   kt_ref, v_ref, w_in_bf_ref, w_out_bf_ref, wk_bf, wv_bf):
    @pl.when(pl.program_id(0) == 0)
    def _():
        _cast_slabs((wk_ref, wv_ref), (wk_bf, wv_bf))

    _cast_slabs((w_in_ref, w_out_ref), (w_in_bf_ref, w_out_bf_ref))
    m = mem_ref[0]
    mn = (m * _rms_scale(m) * g_ref[...]).astype(BF16)
    k = _dot(mn, wk_bf[...])
    kt_ref[0] = jnp.transpose(k).astype(BF16)
    v_ref[0] = _dot(mn, wv_bf[...]).astype(BF16)


ProjSet = collections.namedtuple("ProjSet", "x zq k v qe ke qb kd dec iv sg")


def _proj_set_shapes(tt):
    return [
        pltpu.VMEM((tt, D_MODEL), F32),
        pltpu.VMEM((tt, SWA_WIDTH), BF16),
        pltpu.VMEM((2, WINDOW + tt, LANES), BF16),
        pltpu.VMEM((2, WINDOW + tt, LANES), BF16),
        pltpu.VMEM((tt, HGRN_WIDTH), BF16),
        pltpu.VMEM((tt, HGRN_WIDTH), BF16),
        pltpu.VMEM((tt, HGRN_WIDTH), BF16),
        pltpu.VMEM((tt, HGRN_WIDTH), BF16),
        pltpu.VMEM((tt // CHUNK, HGRN_WIDTH, HGRN_HEAD_DIM), F32),
        pltpu.VMEM((tt, HGRN_WIDTH), BF16),
        pltpu.VMEM((tt, HGRN_WIDTH), F32),
    ]


class _ProjectStage:
    def __init__(self, x_ref, gpre_ref, lbraw_ref, w_in_ref, ps):
        self.x_ref, self.gpre_ref, self.lbraw_ref, self.w_in_ref, self.ps = (
            x_ref, gpre_ref, lbraw_ref, w_in_ref, ps)
        self.tt = x_ref.shape[0]

    def _hgrn_cols(self, i):
        return _dot(self.u, self.w_in_ref[:, HGRN_COL0 + i * HGRN_WIDTH:HGRN_COL0 + (i + 1) * HGRN_WIDTH])

    def norm(self):
        x = self.x_ref[...]
        self.ps.x[...] = x
        self.u = (x * _rms_scale(x) * self.gpre_ref[...]).astype(BF16)

    def swa_q(self):
        zq = _dot(self.u, self.w_in_ref[:, 0:SWA_WIDTH])
        self.ps.zq[...] = (zq * (SWA_HEAD_DIM ** -0.5 * LOG2E)).astype(BF16)

    def swa_kv(self):
        ps = self.ps
        zkv = _dot(self.u, self.w_in_ref[:, SWA_WIDTH:HGRN_COL0])
        k = zkv[:, 0:SWA_KV_WIDTH]
        v = zkv[:, SWA_KV_WIDTH:2 * SWA_KV_WIDTH]
        ps.k[0, WINDOW:, :] = k.astype(BF16)
        ps.k[1, WINDOW:, :] = pltpu.roll(k, SWA_HEAD_DIM, axis=1).astype(BF16)
        ps.v[0, WINDOW:, :] = v.astype(BF16)
        ps.v[1, WINDOW:, :] = pltpu.roll(v, SWA_HEAD_DIM, axis=1).astype(BF16)

    def hgrn_q(self):
        qh = self._hgrn_cols(0)
        self.qf = qh * _sigmoid(qh) * (HGRN_HEAD_DIM ** -0.5)

    def hgrn_f(self):
        a = self.lbraw_ref[...]
        e = jnp.exp(a - jnp.max(a, axis=0, keepdims=True))
        lb = e[0:1] / jnp.sum(e, axis=0, keepdims=True)
        f = lb + (1.0 - lb) * _sigmoid(self._hgrn_cols(1))
        self.kf = 1.0 - f
        self.log2f = jnp.log2(f)

    def hgrn_i(self):
        self.ps.iv[...] = self._hgrn_cols(2).astype(BF16)

    def hgrn_g(self):
        gh = self._hgrn_cols(3)
        self.ps.sg[...] = gh * _sigmoid(gh)

    def hgrn_decay(self):
        ps = self.ps
        cs = 4 * CHUNK
        ri = lax.broadcasted_iota(jnp.int32, (cs, cs), 0)
        ci = lax.broadcasted_iota(jnp.int32, (cs, cs), 1)
        tri = jnp.where((ri // CHUNK == ci // CHUNK) & (ci <= ri), 1.0, 0.0).astype(BF16)
        for rb in range(self.tt // cs):
            lf = self.log2f[rb * cs:(rb + 1) * cs]
            hi = lf.astype(BF16)
            lo = (lf - hi.astype(F32)).astype(BF16)
            b_blk = _dot(tri, hi) + _dot(tri, lo)
            for cc in range(cs // CHUNK):
                c = rb * (cs // CHUNK) + cc
                rows = slice(c * CHUNK, (c + 1) * CHUNK)
                b = b_blk[cc * CHUNK:(cc + 1) * CHUNK]
                b_mid = b[CHUNK // 2 - 1:CHUNK // 2]
                b_last = b[CHUNK - 1:CHUNK]
                qe = self.qf[rows] * jnp.exp2(b - b_mid)
                ke = self.kf[rows] * jnp.exp2(b_mid - b)
                ps.qe[rows, :] = qe.astype(BF16)
                ps.ke[rows, :] = ke.astype(BF16)
                ps.qb[rows, :] = (qe * jnp.exp2(b_mid)).astype(BF16)
                ps.kd[rows, :] = (ke * jnp.exp2(b_last - b_mid)).astype(BF16)
                dec_rows = jnp.broadcast_to(jnp.exp2(b_last), (HGRN_HEAD_DIM, HGRN_WIDTH))
                ps.dec[c] = jnp.transpose(dec_rows)


class _MixStage:
    PAIR_BLOCKS = SWA_WIDTH // LANES
    PAIRS_PER_KV = PAIR_BLOCKS // SWA_KV_HEADS
    N_KEYS = WINDOW + Q_BLOCK

    def __init__(self, first, sinks_ref, onorm_ref, gpost_ref, w_out_ref, ps, kwin, vwin,
                 st_ref, y_ref, out_ref):
        self.first, self.sinks_ref, self.onorm_ref, self.gpost_ref = (
            first, sinks_ref, onorm_ref, gpost_ref)
        self.w_out_ref, self.ps, self.kwin, self.vwin, self.st_ref, self.y_ref, self.out_ref = (
            w_out_ref, ps, kwin, vwin, st_ref, y_ref, out_ref)
        self.tt = out_ref.shape[1]
        self.scores = {}
        self.amat = {}
        self.kv = {}

    def _parity(self, ver, bi):
        return ver ^ (bi // self.PAIRS_PER_KV)

    def swa_window(self):
        ps = self.ps
        zero_win = jnp.zeros((2, WINDOW, LANES), BF16)
        ps.k[:, 0:WINDOW, :] = jnp.where(self.first, zero_win, self.kwin[...])
        ps.v[:, 0:WINDOW, :] = jnp.where(self.first, zero_win, self.vwin[...])
        lane = lax.broadcasted_iota(jnp.int32, (Q_BLOCK, LANES), 1)
        self.low_half = lane < SWA_HEAD_DIM
        row_chunk = lax.broadcasted_iota(jnp.int32, (Q_BLOCK, self.N_KEYS), 0) // CHUNK + WINDOW // CHUNK
        col = lax.broadcasted_iota(jnp.int32, (Q_BLOCK, self.N_KEYS), 1)
        col_chunk = col // CHUNK
        self.band = (col_chunk <= row_chunk) & (col_chunk >= row_chunk - WINDOW // CHUNK)
        self.band_first = self.band & ((col >= WINDOW) | jnp.logical_not(self.first))
        col_q = lax.broadcasted_iota(jnp.int32, (CHUNK, LANES), 1)
        self.keep_low = col_q < CHUNK
        self.keep_high = col_q >= CHUNK

    def _mask_scores(self, j, s):
        if j == 0:
            return jnp.where(self.band_first, s, NEG_INF)
        top = jnp.concatenate([s[0:CHUNK, 0:LANES],
                               jnp.where(self.keep_low, s[0:CHUNK, LANES:2 * LANES], NEG_INF)], axis=1)
        bot = jnp.concatenate([jnp.where(self.keep_high, s[CHUNK:2 * CHUNK, 0:LANES], NEG_INF),
                               s[CHUNK:2 * CHUNK, LANES:2 * LANES]], axis=1)
        return jnp.concatenate([top, bot], axis=0)

    def swa_scores(self, j):
        r0 = j * Q_BLOCK
        qb = self.ps.zq[r0:r0 + Q_BLOCK, :]
        for ver in range(2):
            parts = []
            for bi in range(self.PAIR_BLOCKS):
                keep = self.low_half if self._parity(ver, bi) == 0 else jnp.logical_not(self.low_half)
                parts.append(jnp.where(keep, qb[:, bi * LANES:(bi + 1) * LANES], jnp.zeros((), BF16)))
            qs = jnp.concatenate(parts, axis=0)
            self.scores[j, ver] = _dot_nt(qs, self.ps.k[ver, r0:r0 + self.N_KEYS, :])

    def swa_finish(self, j):
        r0 = j * Q_BLOCK
        halves = [[None, None] for _ in range(self.PAIR_BLOCKS)]
        for ver in range(2):
            s = self.scores.pop((j, ver))
            probs, inv_ls = [], []
            for bi in range(self.PAIR_BLOCKS):
                sink = self.sinks_ref[2 * bi + self._parity(ver, bi)] * LOG2E
                sb = self._mask_scores(j, s[bi * Q_BLOCK:(bi + 1) * Q_BLOCK])
                m = jnp.maximum(jnp.max(sb, axis=-1, keepdims=True), sink)
                p = jnp.exp2(sb - m)
                l = jnp.sum(p, axis=-1, keepdims=True) + jnp.exp2(sink - m)
                probs.append(p.astype(BF16))
                inv_ls.append(1.0 / l)
            o = _dot(jnp.concatenate(probs, axis=0), self.ps.v[ver, r0:r0 + self.N_KEYS, :])
            for bi in range(self.PAIR_BLOCKS):
                halves[bi][self._parity(ver, bi)] = o[bi * Q_BLOCK:(bi + 1) * Q_BLOCK] * inv_ls[bi]
        for bi in range(self.PAIR_BLOCKS):
            blk = jnp.where(self.low_half, halves[bi][0], halves[bi][1])
            self.y_ref[r0:r0 + Q_BLOCK, bi * LANES:(bi + 1) * LANES] = blk.astype(BF16)

    def swa_save_window(self):
        self.kwin[...] = self.ps.k[:, self.tt:self.tt + WINDOW, :]
        self.vwin[...] = self.ps.v[:, self.tt:self.tt + WINDOW, :]

    @staticmethod
    def _chunk(hd, c):
        return slice(c * CHUNK, (c + 1) * CHUNK), slice(hd * HGRN_HEAD_DIM, (hd + 1) * HGRN_HEAD_DIM)

    def hgrn_local(self, pair):
        ps = self.ps
        ri = lax.broadcasted_iota(jnp.int32, (CHUNK, CHUNK), 0)
        ci = lax.broadcasted_iota(jnp.int32, (CHUNK, CHUNK), 1)
        causal = ci <= ri
        hd_a, hd_b = 2 * pair, 2 * pair + 1
        for c in range(self.tt // CHUNK):
            rows, cols_a = self._chunk(hd_a, c)
            _, cols_b = self._chunk(hd_b, c)
            for hd, cols in ((hd_a, cols_a), (hd_b, cols_b)):
                a = _dot_nt(ps.qe[rows, cols], ps.ke[rows, cols])
                self.amat[hd, c] = jnp.where(causal, a, 0.0).astype(BF16)
            for hd, cols in ((hd_a, cols_a), (hd_b, cols_b)):
                self.kv[hd, c] = _dot_tn(ps.kd[rows, cols], ps.iv[rows, cols])

    def hgrn_state(self, hd):
        ps = self.ps
        onorm = self.onorm_ref[...]
        st = self.st_ref[hd] * jnp.where(self.first, 0.0, 1.0)
        for c in range(self.tt // CHUNK):
            rows, cols = self._chunk(hd, c)
            lhs = jnp.concatenate([ps.qb[rows, cols], self.amat.pop((hd, c))], axis=1)
            rhs = jnp.concatenate([st.astype(BF16), ps.iv[rows, cols]], axis=0)
            o = _dot(lhs, rhs)
            st = ps.dec[c, cols, :] * st + self.kv.pop((hd, c))
            on = o * _rms_scale(o) * onorm
            self.y_ref[rows, SWA_WIDTH + hd * HGRN_HEAD_DIM:SWA_WIDTH + (hd + 1) * HGRN_HEAD_DIM] = (
                on * self.ps.sg[rows, cols]).astype(BF16)
        self.st_ref[hd] = st

    def out_swa(self):
        self.y = _dot(self.y_ref[:, 0:SWA_WIDTH], self.w_out_ref[0:SWA_WIDTH, :])

    def out_hgrn(self):
        y = self.y + _dot(self.y_ref[:, SWA_WIDTH:D_MODEL], self.w_out_ref[SWA_WIDTH:D_MODEL, :])
        self.out_ref[0] = self.ps.x[...] + y * _rms_scale(y) * self.gpost_ref[...]


X_BUFFERS = 3


def _mixer_kernel(tiles_per_seq, n_tiles, n_down_slabs, sinks_ref, x_hbm, lbraw_ref, onorm_ref, gpre_ref,
                  gpost_ref, w_in_ref, w_out_ref, wq_ref, wo_ref, wg_ref, wu_ref, wd_ref,
                  out_ref, wq_bf_ref, wo_bf_ref, wg_bf_ref, wu_bf_ref, wd_bf_ref, *scratch):
    n_set = len(ProjSet._fields)
    sets = (ProjSet(*scratch[0:n_set]), ProjSet(*scratch[n_set:2 * n_set]))
    kwin, vwin, st_ref, y_ref, xbuf, xsem = scratch[2 * n_set:]
    s = pl.program_id(0)
    tt = xbuf.shape[1]

    def x_copy(tile):
        slot = lax.rem(tile, X_BUFFERS)
        rows = pl.ds(pl.multiple_of(lax.rem(tile, tiles_per_seq) * tt, tt), tt)
        return pltpu.make_async_copy(x_hbm.at[tile // tiles_per_seq, rows, :], xbuf.at[slot], xsem.at[slot])

    @pl.when(s == 0)
    def _():
        for tile in range(X_BUFFERS - 1):
            x_copy(jnp.int32(tile)).start()

    @pl.when(s < n_tiles)
    def _():
        x_copy(s).wait()

    xa_ref = xbuf.at[lax.rem(jnp.minimum(s, n_tiles - 1), X_BUFFERS)]

    _cast_slabs((wq_ref, wo_ref, wg_ref, wu_ref), (wq_bf_ref, wo_bf_ref, wg_bf_ref, wu_bf_ref))

    @pl.when(s < n_down_slabs)
    def _():
        _cast_slabs((wd_ref,), (wd_bf_ref,))

    @pl.when(s == 0)
    def _():
        for ref in sets[1] + (kwin, vwin, st_ref):
            ref[...] = jnp.zeros_like(ref)

    first = lax.rem(jnp.maximum(s - 1, 0), tiles_per_seq) == 0

    def body(write_set, read_set):
        pj = _ProjectStage(xa_ref, gpre_ref, lbraw_ref, w_in_ref, write_set)
        mx = _MixStage(first, sinks_ref, onorm_ref, gpost_ref, w_out_ref, read_set, kwin, vwin,
                       st_ref, y_ref, out_ref)
        mx.swa_window()
        pj.norm()
        mx.swa_scores(0)
        pj.hgrn_q()
        mx.swa_finish(0)
        mx.swa_scores(2)
        mx.swa_scores(1)
        pj.hgrn_f()
        mx.swa_finish(1)
        pj.hgrn_g()
        mx.swa_scores(3)
        mx.swa_finish(2)
        pj.hgrn_decay()
        mx.swa_finish(3)
        mx.swa_save_window()
        mx.hgrn_local(0)
        mx.hgrn_local(1)
        pj.swa_q()
        mx.hgrn_state(0)
        mx.hgrn_state(1)
        pj.swa_kv()
        mx.hgrn_state(3)
        mx.hgrn_state(2)
        mx.out_swa()
        mx.out_hgrn()
        pj.hgrn_i()

    @pl.when(lax.rem(s, 2) == 0)
    def _():
        body(sets[0], sets[1])

    @pl.when(lax.rem(s, 2) == 1)
    def _():
        body(sets[1], sets[0])

    @pl.when(s + X_BUFFERS - 1 < n_tiles)
    def _():
        x_copy(s + X_BUFFERS - 1).start()


def _xattn_ffn_kernel(h_ref, kt_ref, v_ref, gxpre_ref, gxpost_ref, wq_ref, wo_ref,
                      gfpre_ref, gfpost_ref, wg_ref, wu_ref, wd_ref, out_ref, o_ref):
    half = h_ref.shape[1] // 2
    halves = (slice(0, half), slice(half, 2 * half))
    head_cols = [slice(hd * XATTN_HEAD_DIM, (hd + 1) * XATTN_HEAD_DIM) for hd in range(XATTN_HEADS)]

    def queries(rows):
        h = h_ref[0, rows, :]
        u = (h * _rms_scale(h) * gxpre_ref[...]).astype(BF16)
        return (_dot(u, wq_ref[...]) * (XATTN_HEAD_DIM ** -0.5 * LOG2E)).astype(BF16)

    def scores(q):
        return [_dot(q[:, cols], kt_ref[0, cols, :]) for cols in head_cols]

    def attend(rows, s_heads):
        for cols, s in zip(head_cols, s_heads):
            p = jnp.exp2(s - jnp.max(s, axis=-1, keepdims=True))
            l = jnp.sum(p, axis=-1, keepdims=True)
            o = _dot(p.astype(BF16), v_ref[0, :, cols]) * (1.0 / l)
            o_ref[rows, cols] = o.astype(BF16)

    def attn_out(rows):
        return _dot(o_ref[rows, :], wo_ref[...])

    def attn_residual(rows, y):
        return h_ref[0, rows, :] + y * _rms_scale(y) * gxpost_ref[...]

    def ffn_in(h2):
        return (h2 * _rms_scale(h2) * gfpre_ref[...]).astype(BF16)

    def ffn_residual(rows, h2, y):
        out_ref[0, rows, :] = h2 + y * _rms_scale(y) * gfpost_ref[...]

    q_a = queries(halves[0])
    s_a = scores(q_a)
    q_b = queries(halves[1])
    attend(halves[0], s_a)
    s_b = scores(q_b)
    attend(halves[1], s_b)
    y_a = attn_out(halves[0])
    h2_a = attn_residual(halves[0], y_a)
    y_b = attn_out(halves[1])
    u_a = ffn_in(h2_a)
    up_a = _dot(u_a, wu_ref[...])
    h2_b = attn_residual(halves[1], y_b)
    gate_a = _dot(u_a, wg_ref[...])
    u_b = ffn_in(h2_b)
    act_a = (gate_a * _sigmoid(gate_a) * up_a).astype(BF16)
    z_a = _dot(act_a, wd_ref[...])
    up_b = _dot(u_b, wu_ref[...])
    gate_b = _dot(u_b, wg_ref[...])
    ffn_residual(halves[0], h2_a, z_a)
    act_b = (gate_b * _sigmoid(gate_b) * up_b).astype(BF16)
    z_b = _dot(act_b, wd_ref[...])
    ffn_residual(halves[1], h2_b, z_b)


def _resident(shape):
    return pl.BlockSpec(shape, lambda *_: (0,) * len(shape), pipeline_mode=pl.Buffered(1))


def _params(n_axes):
    return pltpu.CompilerParams(dimension_semantics=("arbitrary",) * n_axes,
                                vmem_limit_bytes=VMEM_LIMIT_BYTES)


def kernel(x, mem, w_in, sinks, hgrn_lb, hgrn_onorm, w_out, g_mix_pre, g_mix_post, g_mem, g_x_pre,
           g_x_post, wq_x, wk_x, wv_x, wo_x, g_ffn_pre, g_ffn_post, w_gate, w_up, w_down):
    B, T, D = x.shape
    assert D == D_MODEL and T % TOKEN_TILE == 0 and mem.shape == (B, MEM_LEN, D_MODEL)
    assert w_in.shape == (1, D_MODEL, D_IN) and hgrn_lb.shape[0] == 2
    tt = TOKEN_TILE
    nt = T // tt
    n_tiles = B * nt

    def row(g):
        return g.reshape(1, -1).astype(F32)

    tok_spec = pl.BlockSpec((1, tt, D_MODEL), lambda b, t: (b, t, 0))
    tok_shape = jax.ShapeDtypeStruct((B, T, D_MODEL), F32)

    def slab_spec(rows, cols, n_slabs):
        return pl.BlockSpec((rows, cols), lambda s: (jnp.minimum(s, n_slabs - 1), 0))

    def bf16_like(w):
        return jax.ShapeDtypeStruct(w.shape, BF16)

    w_in_f, w_out_f = w_in[0].astype(F32), w_out[0].astype(F32)
    in_rows = D_MODEL // B
    kt, v, w_in_bf, w_out_bf = pl.pallas_call(
        _mem_kv_kernel,
        grid=(B,),
        in_specs=[pl.BlockSpec((1, MEM_LEN, D_MODEL), lambda b: (b, 0, 0)),
                  _resident((1, D_MODEL)), _resident((D_MODEL, D_MODEL)), _resident((D_MODEL, D_MODEL)),
                  slab_spec(in_rows, D_IN, B), slab_spec(in_rows, D_MODEL, B)],
        out_specs=[pl.BlockSpec((1, D_MODEL, MEM_LEN), lambda b: (b, 0, 0)),
                   pl.BlockSpec((1, MEM_LEN, D_MODEL), lambda b: (b, 0, 0)),
                   slab_spec(in_rows, D_IN, B), slab_spec(in_rows, D_MODEL, B)],
        out_shape=[jax.ShapeDtypeStruct((B, D_MODEL, MEM_LEN), BF16),
                   jax.ShapeDtypeStruct((B, MEM_LEN, D_MODEL), BF16),
                   bf16_like(w_in_f), bf16_like(w_out_f)],
        scratch_shapes=[pltpu.VMEM((D_MODEL, D_MODEL), BF16), pltpu.VMEM((D_MODEL, D_MODEL), BF16)],
        compiler_params=_params(1),
        name="mem_kv",
    )(mem, row(g_mem[0]), wk_x[0].astype(F32), wv_x[0].astype(F32), w_in_f, w_out_f)

    def project_idx(s):
        i = jnp.minimum(s, n_tiles - 1)
        return (i // nt, i % nt, 0)

    def mix_idx(s):
        i = jnp.maximum(s - 1, 0)
        return (i // nt, i % nt, 0)

    next_w = [w[0].astype(F32) for w in (wq_x, wo_x, w_gate, w_up, w_down)]
    step_rows = D_MODEL // n_tiles
    down_rows = LANES
    n_down_slabs = D_FF // down_rows
    assert step_rows * n_tiles == D_MODEL and step_rows % 16 == 0 and n_down_slabs <= n_tiles
    next_w_specs = [slab_spec(step_rows, D_MODEL, n_tiles), slab_spec(step_rows, D_MODEL, n_tiles),
                    slab_spec(step_rows, D_FF, n_tiles), slab_spec(step_rows, D_FF, n_tiles),
                    slab_spec(down_rows, D_MODEL, n_down_slabs)]

    h1, wq_bf, wo_bf, wg_bf, wu_bf, wd_bf = pl.pallas_call(
        functools.partial(_mixer_kernel, nt, n_tiles, n_down_slabs),
        grid=(n_tiles + 1,),
        in_specs=[pl.BlockSpec(memory_space=pltpu.SMEM),
                  pl.BlockSpec(memory_space=pl.ANY),
                  _resident((2, HGRN_WIDTH)), _resident((1, HGRN_HEAD_DIM)),
                  _resident((1, D_MODEL)), _resident((1, D_MODEL)),
                  _resident((D_MODEL, D_IN)), _resident((D_MODEL, D_MODEL))] + next_w_specs,
        out_specs=[pl.BlockSpec((1, tt, D_MODEL), mix_idx)] + next_w_specs,
        out_shape=[tok_shape] + [bf16_like(w) for w in next_w],
        scratch_shapes=_proj_set_shapes(tt) + _proj_set_shapes(tt) + [
            pltpu.VMEM((2, WINDOW, LANES), BF16),
            pltpu.VMEM((2, WINDOW, LANES), BF16),
            pltpu.VMEM((HGRN_HEADS, HGRN_HEAD_DIM, HGRN_HEAD_DIM), F32),
            pltpu.VMEM((tt, D_MODEL), BF16),
            pltpu.VMEM((X_BUFFERS, tt, D_MODEL), F32),
            pltpu.SemaphoreType.DMA((X_BUFFERS,)),
        ],
        compiler_params=_params(1),
        name="mixer",
    )(sinks[0].astype(F32), x, hgrn_lb.astype(F32), row(hgrn_onorm[0]), row(g_mix_pre[0]),
      row(g_mix_post[0]), w_in_bf, w_out_bf, *next_w)

    h3 = pl.pallas_call(
        _xattn_ffn_kernel,
        grid=(B, nt),
        in_specs=[tok_spec,
                  pl.BlockSpec((1, D_MODEL, MEM_LEN), lambda b, t: (b, 0, 0)),
                  pl.BlockSpec((1, MEM_LEN, D_MODEL), lambda b, t: (b, 0, 0)),
                  _resident((1, D_MODEL)), _resident((1, D_MODEL)),
                  _resident((D_MODEL, D_MODEL)), _resident((D_MODEL, D_MODEL)),
                  _resident((1, D_MODEL)), _resident((1, D_MODEL)),
                  _resident((D_MODEL, D_FF)), _resident((D_MODEL, D_FF)), _resident((D_FF, D_MODEL))],
        out_specs=tok_spec,
        out_shape=tok_shape,
        scratch_shapes=[pltpu.VMEM((tt, D_MODEL), BF16)],
        compiler_params=_params(2),
        name="xattn_ffn",
    )(h1, kt, v, row(g_x_pre[0]), row(g_x_post[0]), wq_bf, wo_bf,
      row(g_ffn_pre[0]), row(g_ffn_post[0]), wg_bf, wu_bf, wd_bf)
    return h3
```
